```python
import math
import jax, jax.numpy as jnp
from jax import lax
import numpy as np

D_MODEL = 1024
BATCH = 16
SEQ = 2048
DEPTH = 4

GRID_W = 64
CTX_LEN = 256
HEAD_DIM = 64
A_HEADS = D_MODEL // 128
A_KV_HEADS = 2
A_GROUP = A_HEADS // A_KV_HEADS
CONV_CH = D_MODEL // 4
CONV_WIDTH = 3
C_HEADS = D_MODEL // 256
C_QK_DIM = 32
C_V_DIM = 2 * C_QK_DIM
MIX_A = A_HEADS * HEAD_DIM
MIX_B = CONV_CH
MIX_C = C_HEADS * C_V_DIM
MIX_WIDTH = MIX_A + MIX_B + MIX_C
IN_SPLITS = (MIX_A, A_KV_HEADS * HEAD_DIM, A_KV_HEADS * HEAD_DIM,
             CONV_CH, CONV_CH, CONV_CH,
             C_HEADS * 2 * C_QK_DIM, C_HEADS * 2 * C_QK_DIM, MIX_C)
IN_WIDTH = sum(IN_SPLITS)
IN_OFFSETS = tuple(int(v) for v in np.cumsum(IN_SPLITS)[:-1])
D_FF = ((8 * D_MODEL // 3 + 127) // 128) * 128
N_EXPERTS = 8
TOP_K = 2
N_DENSE = (DEPTH + 1) // 2
N_MOE = DEPTH // 2
Q_BLOCK = 128
ROPE_THETA = 10000.0
EPS = 1e-6

kernel_name = "hybrid_dit_parallel_heads_moe"


def rms_norm(x, g):
    xf = x.astype(jnp.float32)
    y = xf * lax.rsqrt(jnp.mean(xf * xf, axis=-1, keepdims=True) + EPS)
    return (y * g.astype(jnp.float32)).astype(x.dtype)


def rope_1d(x, pos):
    n = x.shape[-1]
    inv = ROPE_THETA ** (-jnp.arange(0, n, 2, dtype=jnp.float32) / n)
    ang = pos.astype(jnp.float32)[:, None] * inv[None, :]
    shape = (1, x.shape[1]) + (1,) * (x.ndim - 3) + (n // 2,)
    cos = jnp.cos(ang).reshape(shape)
    sin = jnp.sin(ang).reshape(shape)
    xf = x.astype(jnp.float32)
    x1, x2 = xf[..., : n // 2], xf[..., n // 2:]
    return jnp.concatenate([x1 * cos - x2 * sin, x2 * cos + x1 * sin], axis=-1).astype(x.dtype)


def axial_rope(x, row, col):
    half = x.shape[-1] // 2
    return jnp.concatenate([rope_1d(x[..., :half], row), rope_1d(x[..., half:], col)], axis=-1)


def gqa_attend(q, k, v):
    s = jnp.einsum('bqkgd,bskd->bkgqs', q, k).astype(jnp.float32) * (q.shape[-1] ** -0.5)
    p = jax.nn.softmax(s, axis=-1).astype(v.dtype)
    return jnp.einsum('bkgqs,bskd->bqkgd', p, v)


def diff_attend(q, k, v, lam):
    s = jnp.einsum('bqhcd,bshcd->bhcqs', q, k).astype(jnp.float32) * (q.shape[-1] ** -0.5)
    p = jax.nn.softmax(s, axis=-1)
    a = p[:, :, 0] - lam * p[:, :, 1]
    return jnp.einsum('bhqs,bshd->bqhd', a.astype(v.dtype), v)


def sweep_query_blocks(fn, q):
    b, n = q.shape[:2]
    qb = jnp.moveaxis(q.reshape((b, n // Q_BLOCK, Q_BLOCK) + q.shape[2:]), 1, 0)
    out = jnp.moveaxis(lax.map(fn, qb), 0, 1)
    return out.reshape((b, n) + out.shape[3:])


def short_conv(z, w):
    pad = (CONV_WIDTH - 1) // 2
    n = z.shape[1]
    zp = jnp.pad(z, ((0, 0), (pad, CONV_WIDTH - 1 - pad), (0, 0)))
    y = w[0] * zp[:, 0:n]
    for j in range(1, CONV_WIDTH):
        y = y + w[j] * zp[:, j:j + n]
    return y


def mixer_inputs(p, a_qg, a_kg, d_qg, d_kg, pos):
    b, n = p.shape[:2]
    aq, ak, av, bu, bb, bc, dq, dk, dv = jnp.split(p, IN_OFFSETS, axis=-1)
    aq = rms_norm(aq.reshape(b, n, A_KV_HEADS, A_GROUP, HEAD_DIM), a_qg)
    ak = rms_norm(ak.reshape(b, n, A_KV_HEADS, HEAD_DIM), a_kg)
    av = av.reshape(b, n, A_KV_HEADS, HEAD_DIM)
    dq = rms_norm(dq.reshape(b, n, C_HEADS, 2, C_QK_DIM), d_qg)
    dk = rms_norm(dk.reshape(b, n, C_HEADS, 2, C_QK_DIM), d_kg)
    dv = dv.reshape(b, n, C_HEADS, C_V_DIM)
    if pos is not None:
        row, col = pos
        aq = axial_rope(aq, row, col)
        ak = axial_rope(ak, row, col)
        dq = axial_rope(dq, row, col)
        dk = axial_rope(dk, row, col)
    return aq, ak, av, bu, bb, bc, dq, dk, dv


def merge_heads(a, bo, d, subln, lam_init, w_o):
    b, n = bo.shape[:2]
    d = rms_norm(d, subln) * (1.0 - lam_init)
    y = jnp.concatenate([a.reshape(b, n, MIX_A), bo, d.reshape(b, n, MIX_C)], axis=-1)
    return y @ w_o


def swiglu(h, wg, wu, wd):
    return (jax.nn.silu(h @ wg) * (h @ wu)) @ wd


def moe_swiglu(h, rw, rb, wg, wu, wd):
    logits = (h @ rw).astype(jnp.float32) + rb.astype(jnp.float32)
    top_v, top_i = lax.top_k(logits, TOP_K)
    gates = jax.nn.softmax(top_v, axis=-1)
    combine = jnp.sum(jax.nn.one_hot(top_i, N_EXPERTS, dtype=jnp.float32) * gates[..., None], axis=-2)
    combine = combine.astype(h.dtype)
    out = jnp.zeros_like(h)
    for e in range(N_EXPERTS):
        out = out + combine[..., e:e + 1] * swiglu(h, wg[e], wu[e], wd[e])
    return out


def channel_mix(h, l, dense_w_gate, dense_w_up, dense_w_down, router_w, router_b,
                moe_w_gate, moe_w_up, moe_w_down):
    i = l // 2
    if l % 2 == 0:
        return swiglu(h, dense_w_gate[i], dense_w_up[i], dense_w_down[i])
    return moe_swiglu(h, router_w[i], router_b[i], moe_w_gate[i], moe_w_up[i], moe_w_down[i])


def setup_inputs(seed: int = 0) -> dict:
    key = jax.random.key(seed)
    ks = jax.random.split(key, 32)
    f32 = jnp.float32
    nrm = lambda k, shape, s: jax.random.normal(k, shape, f32) * s
    D = D_MODEL
    return {
        "x": nrm(ks[0], (BATCH, SEQ, D), 1.0),
        "c": nrm(ks[1], (BATCH, D), 1.0),
        "ctx": nrm(ks[2], (BATCH, CTX_LEN, D), 1.0),
        "c_ctx": nrm(ks[3], (D,), 1.0),
        "w_ada": nrm(ks[4], (DEPTH, D, 6 * D), 0.5 * D ** -0.5),
        "b_ada": nrm(ks[5], (DEPTH, 6 * D), 0.02),
        "norm_mix": 1.0 + nrm(ks[6], (DEPTH, D), 0.02),
        "norm_ffn": 1.0 + nrm(ks[7], (DEPTH, D), 0.02),
        "w_in": nrm(ks[8], (DEPTH, D, IN_WIDTH), D ** -0.5),
        "w_out": nrm(ks[9], (DEPTH, MIX_WIDTH, D), MIX_WIDTH ** -0.5),
        "a_q_gain": 1.0 + nrm(ks[10], (DEPTH, HEAD_DIM), 0.02),
        "a_k_gain": 1.0 + nrm(ks[11], (DEPTH, HEAD_DIM), 0.02),
        "d_q_gain": 1.0 + nrm(ks[12], (DEPTH, C_QK_DIM), 0.02),
        "d_k_gain": 1.0 + nrm(ks[13], (DEPTH, C_QK_DIM), 0.02),
        "conv_w": nrm(ks[14], (DEPTH, CONV_WIDTH, CONV_CH), CONV_WIDTH ** -0.5),
        "diff_lambda": nrm(ks[15], (DEPTH, 4, C_QK_DIM), 0.1),
        "diff_subln": 1.0 + nrm(ks[16], (DEPTH, C_V_DIM), 0.02),
        "dense_w_gate": nrm(ks[17], (N_DENSE, D, D_FF), D ** -0.5),
        "dense_w_up": nrm(ks[18], (N_DENSE, D, D_FF), D ** -0.5),
        "dense_w_down": nrm(ks[19], (N_DENSE, D_FF, D), D_FF ** -0.5),
        "router_w": nrm(ks[20], (N_MOE, D, N_EXPERTS), D ** -0.5),
        "router_b": nrm(ks[21], (N_MOE, N_EXPERTS), 0.01),
        "moe_w_gate": nrm(ks[22], (N_MOE, N_EXPERTS, D, D_FF), D ** -0.5),
        "moe_w_up": nrm(ks[23], (N_MOE, N_EXPERTS, D, D_FF), D ** -0.5),
        "moe_w_down": nrm(ks[24], (N_MOE, N_EXPERTS, D_FF, D), D_FF ** -0.5),
    }


def reference(x, c, ctx, c_ctx, w_ada, b_ada, norm_mix, norm_ffn, w_in, w_out,
              a_q_gain, a_k_gain, d_q_gain, d_k_gain, conv_w, diff_lambda, diff_subln,
              dense_w_gate, dense_w_up, dense_w_down, router_w, router_b,
              moe_w_gate, moe_w_up, moe_w_down):
    n_tok = x.shape[1]
    ROWS = n_tok // GRID_W
    row = jnp.repeat(jnp.arange(ROWS, dtype=jnp.int32), GRID_W)
    col = jnp.tile(jnp.arange(GRID_W, dtype=jnp.int32), ROWS)
    s_lat = jax.nn.silu(c)
    s_ctx = jax.nn.silu(c_ctx)
    xc = ctx
    for l in range(DEPTH):
        last = l == DEPTH - 1
        lam_init = 0.8 - 0.6 * math.exp(-0.3 * l)
        mod = (s_lat @ w_ada[l] + b_ada[l])[:, None, :]
        mod_c = s_ctx @ w_ada[l] + b_ada[l]
        sh1, sc1, g1, sh2, sc2, g2 = jnp.split(mod, 6, axis=-1)
        csh1, csc1, cg1, csh2, csc2, cg2 = jnp.split(mod_c, 6, axis=-1)
        lam_p = diff_lambda[l].astype(jnp.float32)
        lam = (jnp.exp(jnp.sum(lam_p[0] * lam_p[1])) - jnp.exp(jnp.sum(lam_p[2] * lam_p[3]))
               + lam_init)

        h = rms_norm(x, norm_mix[l]) * (1.0 + sc1) + sh1
        hc = rms_norm(xc, norm_mix[l]) * (1.0 + csc1) + csh1
        aq, ak, av, bu, bb, bc, dq, dk, dv = mixer_inputs(
            h @ w_in[l], a_q_gain[l], a_k_gain[l], d_q_gain[l], d_k_gain[l], (row, col))
        caq, cak, cav, cbu, cbb, cbc, cdq, cdk, cdv = mixer_inputs(
            hc @ w_in[l], a_q_gain[l], a_k_gain[l], d_q_gain[l], d_k_gain[l], None)
        ak_all = jnp.concatenate([ak, cak], axis=1)
        av_all = jnp.concatenate([av, cav], axis=1)
        dk_all = jnp.concatenate([dk, cdk], axis=1)
        dv_all = jnp.concatenate([dv, cdv], axis=1)
        a_out = sweep_query_blocks(lambda qb: gqa_attend(qb, ak_all, av_all), aq)
        d_out = sweep_query_blocks(lambda qb: diff_attend(qb, dk_all, dv_all, lam), dq)
        b_out = bb * short_conv(bc * bu, conv_w[l])
        x = x + g1 * merge_heads(a_out, b_out, d_out, diff_subln[l], lam_init, w_out[l])
        if not last:
            ca = gqa_attend(caq, cak, cav)
            cd = diff_attend(cdq, cdk, cdv, lam)
            cb = cbb * short_conv(cbc * cbu, conv_w[l])
            xc = xc + cg1 * merge_heads(ca, cb, cd, diff_subln[l], lam_init, w_out[l])

        h2 = rms_norm(x, norm_ffn[l]) * (1.0 + sc2) + sh2
        x = x + g2 * channel_mix(h2, l, dense_w_gate, dense_w_up, dense_w_down,
                                 router_w, router_b, moe_w_gate, moe_w_up, moe_w_down)
        if not last:
            hc2 = rms_norm(xc, norm_ffn[l]) * (1.0 + csc2) + csh2
            xc = xc + cg2 * channel_mix(hc2, l, dense_w_gate, dense_w_up, dense_w_down,
                                        router_w, router_b, moe_w_gate, moe_w_up, moe_w_down)
    return x
```

```python
import functools
import math

import jax
import jax.numpy as jnp
from jax import lax
from jax.experimental import pallas as pl
from jax.experimental.pallas import tpu as pltpu

F32 = jnp.float32
BF16 = jnp.bfloat16

GRID_W = 64
HEAD_DIM = 64
A_KV_HEADS = 2
CONV_CH = 256
C_HEADS = 4
C_QK_DIM = 32
N_EXPERTS = 8
ROPE_THETA = 10000.0
EPS = 1e-6
LOG2E = 1.4426950408889634

LANES = 128
TOK_TILE = 256
VMEM_LIMIT = 56 * 1024 * 1024


def _cparams(n_axes):
    return pltpu.CompilerParams(
        dimension_semantics=("arbitrary",) * n_axes, vmem_limit_bytes=VMEM_LIMIT)


def _group_sum_matrix(group):
    r = lax.broadcasted_iota(jnp.int32, (LANES, LANES), 0)
    c = lax.broadcasted_iota(jnp.int32, (LANES, LANES), 1)
    shift = int(math.log2(group))
    same = (r >> shift) == (c >> shift)
    return jnp.where(same, 1.0, 0.0).astype(BF16)


def _group_rms(xc, gmat, group):
    ss = jnp.dot((xc * xc).astype(BF16), gmat, preferred_element_type=F32)
    return xc * lax.rsqrt(ss * (1.0 / group) + EPS)


def _ada_body(s_ref, w_ref, b_ref, o_ref):
    s = s_ref[...]
    s = s * (1.0 / (1.0 + jnp.exp(-s)))
    o_ref[...] = jnp.dot(s.astype(BF16), w_ref[...].astype(BF16),
                         preferred_element_type=F32) + b_ref[...]


def _ada_call(cond, w_ada, b_ada):
    depth, d, n = w_ada.shape
    r = cond.shape[0]
    tn = 1024
    return pl.pallas_call(
        _ada_body,
        grid=(depth, n // tn),
        in_specs=[
            pl.BlockSpec((r, d), lambda l, j: (0, 0)),
            pl.BlockSpec((None, d, tn), lambda l, j: (l, 0, j)),
            pl.BlockSpec((None, 1, tn), lambda l, j: (l, 0, j)),
        ],
        out_specs=pl.BlockSpec((None, r, tn), lambda l, j: (l, 0, j)),
        out_shape=jax.ShapeDtypeStruct((depth, r, n), F32),
        compiler_params=_cparams(2),
        name="ada_mod",
    )(cond, w_ada, b_ada)


def _pre_attn_body(x_ref, mod_ref, nw_ref, w_ref, gain_ref, rope_ref,
                   q_ref, kt_ref, v_ref, cv_ref):
    x = x_ref[...]
    h = x * lax.rsqrt(jnp.mean(x * x, axis=-1, keepdims=True) + EPS) * nw_ref[...]
    h = h * (1.0 + mod_ref[1:2, :]) + mod_ref[0:1, :]
    p = jnp.dot(h.astype(BF16), w_ref[...], preferred_element_type=F32)

    g64 = _group_sum_matrix(64)
    g32 = _group_sum_matrix(32)
    lane = lax.broadcasted_iota(jnp.int32, (x.shape[0], LANES), 1)

    def rope(xn, t0, half):
        return (xn * rope_ref[t0] + pltpu.roll(xn, LANES - half, 1) * rope_ref[t0 + 1]
                + pltpu.roll(xn, half, 1) * rope_ref[t0 + 2])

    def chunk(col):
        return p[:, col:col + LANES]

    for c in range(4):
        xn = _group_rms(chunk(c * LANES), g64, 64) * gain_ref[0:1, :]
        q_ref[:, c * LANES:(c + 1) * LANES] = rope(xn, 0, 16).astype(BF16)
    ka = rope(_group_rms(chunk(512), g64, 64) * gain_ref[1:2, :], 0, 16)
    kat = ka.T.astype(BF16)
    kt_ref[0:64, :] = kat[0:64]
    kt_ref[64:128, :] = kat[0:64]
    kt_ref[128:192, :] = kat[64:128]
    kt_ref[192:256, :] = kat[64:128]
    av = chunk(640)
    av_sw = pltpu.roll(av, 64, 1)
    v_ref[:, 0:LANES] = jnp.where(lane < 64, av, av_sw).astype(BF16)
    v_ref[:, LANES:2 * LANES] = jnp.where(lane < 64, av_sw, av).astype(BF16)
    cv_ref[:, 0:CONV_CH] = p[:, 1280:1536] * p[:, 768:1024]
    cv_ref[:, CONV_CH:2 * CONV_CH] = p[:, 1024:1280]
    for c in range(2):
        xn = _group_rms(chunk(1536 + c * LANES), g32, 32) * gain_ref[2:3, :]
        q_ref[:, 512 + c * LANES:512 + (c + 1) * LANES] = rope(xn, 3, 8).astype(BF16)
        kc = rope(_group_rms(chunk(1792 + c * LANES), g32, 32) * gain_ref[3:4, :], 3, 8)
        kt_ref[256 + c * LANES:256 + (c + 1) * LANES, :] = kc.T.astype(BF16)
    v_ref[:, 256:512] = p[:, 2048:2304].astype(BF16)


def _pre_attn_call(xa, mod, nw, w_in, gains, rope):
    b, t, d = xa.shape
    nt = t // TOK_TILE
    n_in = w_in.shape[1]
    return pl.pallas_call(
        _pre_attn_body,
        grid=(b, nt),
        in_specs=[
            pl.BlockSpec((None, TOK_TILE, d), lambda bi, i: (bi, i, 0)),
            pl.BlockSpec((None, None, 6, d), lambda bi, i: (bi, i // (nt - 1), 0, 0)),
            pl.BlockSpec((1, d), lambda bi, i: (0, 0)),
            pl.BlockSpec((d, n_in), lambda bi, i: (0, 0)),
            pl.BlockSpec((8, LANES), lambda bi, i: (0, 0)),
            pl.BlockSpec((6, TOK_TILE, LANES), lambda bi, i: (0, i, 0)),
        ],
        out_specs=[
            pl.BlockSpec((None, TOK_TILE, 768), lambda bi, i: (bi, i, 0)),
            pl.BlockSpec((None, 512, TOK_TILE), lambda bi, i: (bi, 0, i)),
            pl.BlockSpec((None, TOK_TILE, 512), lambda bi, i: (bi, i, 0)),
            pl.BlockSpec((None, TOK_TILE, 512), lambda bi, i: (bi, i, 0)),
        ],
        out_shape=[
            jax.ShapeDtypeStruct((b, t, 768), BF16),
            jax.ShapeDtypeStruct((b, 512, t), BF16),
            jax.ShapeDtypeStruct((b, t, 512), BF16),
            jax.ShapeDtypeStruct((b, t, 512), F32),
        ],
        compiler_params=_cparams(2),
        name="pre_attn",
    )(xa, mod, nw, w_in, gains, rope)


def _softmax_parts(qm, kt):
    s = jnp.dot(qm, kt, preferred_element_type=F32)
    e = jnp.exp2(s - jnp.max(s, axis=1, keepdims=True))
    return e, jnp.sum(e, axis=1, keepdims=True)


def _attn_body(seq, lam_init, q_ref, kt_ref, v_ref, cv_ref, cvp_ref, cvn_ref, x_ref, mod_ref,
               wo_ref, qmask_ref, misc_ref, lamp_ref, o_ref):
    i = pl.program_id(1)
    t = kt_ref.shape[1]
    tq = q_ref.shape[0]
    n_lat = seq // tq
    lane = lax.broadcasted_iota(jnp.int32, (tq, LANES), 1)
    lamp = lamp_ref[...]
    lam = (jnp.exp(jnp.sum(lamp[0:1] * lamp[1:2], axis=1, keepdims=True))
           - jnp.exp(jnp.sum(lamp[2:3] * lamp[3:4], axis=1, keepdims=True)) + lam_init)

    def mixers(k0, nk):
        ys = []
        for c in range(4):
            kv = c // 2
            qc = q_ref[:, c * LANES:(c + 1) * LANES]
            kt = kt_ref[kv * LANES:(kv + 1) * LANES, k0:k0 + nk]
            vd = v_ref[k0:k0 + nk, kv * LANES:(kv + 1) * LANES]
            outs = []
            for j in range(2):
                e, l = _softmax_parts(qc * qmask_ref[j:j + 1, :], kt)
                o = jnp.dot(e.astype(BF16), vd, preferred_element_type=F32)
                outs.append(o * (1.0 / l))
            ys.append(jnp.where(lane < 64, outs[0], outs[1]))
        u = cv_ref[:, 0:CONV_CH]
        row = lax.broadcasted_iota(jnp.int32, (tq, CONV_CH), 0)
        has_prev = jnp.where((i == 0) | (i == n_lat), 0.0, 1.0)
        has_next = jnp.where(i >= n_lat - 1, 0.0, 1.0)
        u_prev = jnp.where(row == 0, cvp_ref[7:8, 0:CONV_CH] * has_prev, pltpu.roll(u, 1, 0))
        u_next = jnp.where(row == tq - 1, cvn_ref[0:1, 0:CONV_CH] * has_next,
                           pltpu.roll(u, tq - 1, 0))
        conv = (misc_ref[0:1, :] * u_prev + misc_ref[1:2, :] * u + misc_ref[2:3, :] * u_next)
        b_out = cv_ref[:, CONV_CH:2 * CONV_CH] * conv
        ys.append(b_out[:, 0:LANES])
        ys.append(b_out[:, LANES:2 * LANES])
        g64 = _group_sum_matrix(64)
        for c in range(2):
            qc = q_ref[:, 512 + c * LANES:512 + (c + 1) * LANES]
            kt = kt_ref[256 + c * LANES:256 + (c + 1) * LANES, k0:k0 + nk]
            vd = v_ref[k0:k0 + nk, 256 + c * LANES:256 + (c + 1) * LANES]
            outs = []
            for hh in range(2):
                e0, l0 = _softmax_parts(qc * qmask_ref[2 + 2 * hh:3 + 2 * hh, :], kt)
                e1, l1 = _softmax_parts(qc * qmask_ref[3 + 2 * hh:4 + 2 * hh, :], kt)
                a = e0 * (1.0 / l0) - e1 * (lam / l1)
                outs.append(jnp.dot(a.astype(BF16), vd, preferred_element_type=F32))
            yc = jnp.where(lane < 64, outs[0], outs[1])
            ys.append(_group_rms(yc, g64, 64) * misc_ref[3:4, 0:LANES] * (1.0 - lam_init))
        y = jnp.concatenate(ys, axis=1).astype(BF16)
        o = jnp.dot(y, wo_ref[...], preferred_element_type=F32)
        o_ref[...] = x_ref[...] + mod_ref[2:3, :] * o

    @pl.when(i < n_lat)
    def _():
        mixers(0, t)

    @pl.when(i >= n_lat)
    def _():
        mixers(seq, t - seq)


def _attn_call(xa, q, kt, v, cv, mod, wo, qmask, misc, lamp, seq, lam_init, n_tiles):
    b, t, d = xa.shape
    nt_all = t // TOK_TILE
    n8 = t // 8
    blk8 = TOK_TILE // 8
    return pl.pallas_call(
        functools.partial(_attn_body, seq, lam_init),
        grid=(b, n_tiles),
        in_specs=[
            pl.BlockSpec((None, TOK_TILE, 768), lambda bi, i: (bi, i, 0)),
            pl.BlockSpec((None, 512, t), lambda bi, i: (bi, 0, 0)),
            pl.BlockSpec((None, t, 512), lambda bi, i: (bi, 0, 0)),
            pl.BlockSpec((None, TOK_TILE, 512), lambda bi, i: (bi, i, 0)),
            pl.BlockSpec((None, 8, 512), lambda bi, i: (bi, jnp.maximum(i * blk8 - 1, 0), 0)),
            pl.BlockSpec((None, 8, 512), lambda bi, i: (bi, jnp.minimum((i + 1) * blk8, n8 - 1), 0)),
            pl.BlockSpec((None, TOK_TILE, d), lambda bi, i: (bi, i, 0)),
            pl.BlockSpec((None, None, 6, d), lambda bi, i: (bi, i // (nt_all - 1), 0, 0)),
            pl.BlockSpec((d, d), lambda bi, i: (0, 0)),
            pl.BlockSpec((8, LANES), lambda bi, i: (0, 0)),
            pl.BlockSpec((8, CONV_CH), lambda bi, i: (0, 0)),
            pl.BlockSpec((4, C_QK_DIM), lambda bi, i: (0, 0)),
        ],
        out_specs=pl.BlockSpec((None, TOK_TILE, d), lambda bi, i: (bi, i, 0)),
        out_shape=jax.ShapeDtypeStruct((b, n_tiles * TOK_TILE, d), F32),
        compiler_params=_cparams(2),
        name="attn_merge",
    )(q, kt, v, cv, cv, cv, xa, mod, wo, qmask, misc, lamp)


def _mod_norm(x, mod_ref, nw_ref):
    h = x * lax.rsqrt(jnp.mean(x * x, axis=-1, keepdims=True) + EPS) * nw_ref[...]
    return h * (1.0 + mod_ref[4:5, :]) + mod_ref[3:4, :]


def _swiglu(h, wg_ref, wu_ref, wd_ref):
    g = jnp.dot(h, wg_ref[...], preferred_element_type=F32)
    u = jnp.dot(h, wu_ref[...], preferred_element_type=F32)
    a = g * (1.0 / (1.0 + jnp.exp(-g))) * u
    return jnp.dot(a.astype(BF16), wd_ref[...], preferred_element_type=F32)


def _dense_ffn_body(x_ref, mod_ref, nw_ref, wg_ref, wu_ref, wd_ref, o_ref):
    x = x_ref[...]
    h = _mod_norm(x, mod_ref, nw_ref).astype(BF16)
    o_ref[...] = x + mod_ref[5:6, :] * _swiglu(h, wg_ref, wu_ref, wd_ref)


def _dense_ffn_call(xa, mod, nw, wg, wu, wd, nt_all):
    b, t, d = xa.shape
    nt = t // TOK_TILE
    f = wg.shape[1]
    return pl.pallas_call(
        _dense_ffn_body,
        grid=(b, nt),
        in_specs=[
            pl.BlockSpec((None, TOK_TILE, d), lambda bi, i: (bi, i, 0)),
            pl.BlockSpec((None, None, 6, d), lambda bi, i: (bi, i // (nt_all - 1), 0, 0)),
            pl.BlockSpec((1, d), lambda bi, i: (0, 0)),
            pl.BlockSpec((d, f), lambda bi, i: (0, 0)),
            pl.BlockSpec((d, f), lambda bi, i: (0, 0)),
            pl.BlockSpec((f, d), lambda bi, i: (0, 0)),
        ],
        out_specs=pl.BlockSpec((None, TOK_TILE, d), lambda bi, i: (bi, i, 0)),
        out_shape=jax.ShapeDtypeStruct((b, t, d), F32),
        compiler_params=_cparams(2),
        name="dense_ffn",
    )(xa, mod, nw, wg, wu, wd)


def _router_body(x_ref, mod_ref, nw_ref, rw_ref, rb_ref, h_ref, comb_ref):
    h = _mod_norm(x_ref[...], mod_ref, nw_ref)
    h_hi = h.astype(BF16)
    h_lo = (h - h_hi.astype(F32)).astype(BF16)
    rw = rw_ref[...]
    rw_hi = rw.astype(BF16)
    rw_lo = (rw - rw_hi.astype(F32)).astype(BF16)
    logits = (jnp.dot(h_hi, rw_hi, preferred_element_type=F32)
              + (jnp.dot(h_hi, rw_lo, preferred_element_type=F32)
                 + jnp.dot(h_lo, rw_hi, preferred_element_type=F32))) + rb_ref[...]
    idx = lax.broadcasted_iota(jnp.int32, logits.shape, 1).astype(F32)
    neg = jnp.float32(-jnp.inf)
    logits = jnp.where(idx < N_EXPERTS, logits, neg)
    m1 = jnp.max(logits, axis=1, keepdims=True)
    i1 = jnp.min(jnp.where(logits == m1, idx, float(LANES)), axis=1, keepdims=True)
    rest = jnp.where(idx == i1, neg, logits)
    m2 = jnp.max(rest, axis=1, keepdims=True)
    i2 = jnp.min(jnp.where(rest == m2, idx, float(LANES)), axis=1, keepdims=True)
    e2 = jnp.exp(m2 - m1)
    den = 1.0 / (1.0 + e2)
    comb_ref[...] = jnp.where(idx == i1, den, 0.0) + jnp.where(idx == i2, e2 * den, 0.0)
    h_ref[...] = h_hi


def _router_call(xa, mod, nw, rw, rb, nt_all):
    b, t, d = xa.shape
    nt = t // TOK_TILE
    return pl.pallas_call(
        _router_body,
        grid=(b, nt),
        in_specs=[
            pl.BlockSpec((None, TOK_TILE, d), lambda bi, i: (bi, i, 0)),
            pl.BlockSpec((None, None, 6, d), lambda bi, i: (bi, i // (nt_all - 1), 0, 0)),
            pl.BlockSpec((1, d), lambda bi, i: (0, 0)),
            pl.BlockSpec((d, LANES), lambda bi, i: (0, 0)),
            pl.BlockSpec((1, LANES), lambda bi, i: (0, 0)),
        ],
        out_specs=[
            pl.BlockSpec((None, TOK_TILE, d), lambda bi, i: (bi, i, 0)),
            pl.BlockSpec((None, TOK_TILE, LANES), lambda bi, i: (bi, i, 0)),
        ],
        out_shape=[
            jax.ShapeDtypeStruct((b, t, d), BF16),
            jax.ShapeDtypeStruct((b, t, LANES), F32),
        ],
        compiler_params=_cparams(2),
        name="router",
    )(xa, mod, nw, rw, rb)


def _expert_ffn_body(e, h_ref, comb_ref, acc_ref, mod_ref, wg_ref, wu_ref, wd_ref, o_ref):
    y = _swiglu(h_ref[...], wg_ref, wu_ref, wd_ref)
    gate = comb_ref[:, e:e + 1]
    o_ref[...] = acc_ref[...] + (mod_ref[5:6, :] * gate) * y


def _expert_ffn_call(e, h, comb, acc, mod, wg, wu, wd, nt_all):
    b, t, d = acc.shape
    nt = t // TOK_TILE
    f = wg.shape[1]
    return pl.pallas_call(
        functools.partial(_expert_ffn_body, e),
        grid=(b, nt),
        in_specs=[
            pl.BlockSpec((None, TOK_TILE, d), lambda bi, i: (bi, i, 0)),
            pl.BlockSpec((None, TOK_TILE, LANES), lambda bi, i: (bi, i, 0)),
            pl.BlockSpec((None, TOK_TILE, d), lambda bi, i: (bi, i, 0)),
            pl.BlockSpec((None, None, 6, d), lambda bi, i: (bi, i // (nt_all - 1), 0, 0)),
            pl.BlockSpec((d, f), lambda bi, i: (0, 0)),
            pl.BlockSpec((d, f), lambda bi, i: (0, 0)),
            pl.BlockSpec((f, d), lambda bi, i: (0, 0)),
        ],
        out_specs=pl.BlockSpec((None, TOK_TILE, d), lambda bi, i: (bi, i, 0)),
        out_shape=jax.ShapeDtypeStruct((b, t, d), F32),
        input_output_aliases={2: 0},
        compiler_params=_cparams(2),
        name="expert_ffn",
    )(h, comb, acc, mod, wg, wu, wd)


def _rope_tables(seq, ctx_len):
    pos_row = jnp.arange(seq, dtype=jnp.int32) // GRID_W
    pos_col = jnp.arange(seq, dtype=jnp.int32) % GRID_W

    def tables(width):
        n = width // 2
        half = n // 2
        inv = ROPE_THETA ** (-jnp.arange(0, n, 2, dtype=F32) / n)
        d = jnp.arange(LANES)
        r = d % n
        first = (r < half)[None, :]
        is_col = ((d % width) // n == 1)[None, :]
        pos = jnp.where(is_col, pos_col[:, None], pos_row[:, None]).astype(F32)
        ang = pos * inv[r % half][None, :]
        cos, sin = jnp.cos(ang), jnp.sin(ang)
        tabs = [cos, jnp.where(first, -sin, 0.0), jnp.where(first, 0.0, sin)]
        ident = [jnp.ones((ctx_len, LANES), F32), jnp.zeros((ctx_len, LANES), F32),
                 jnp.zeros((ctx_len, LANES), F32)]
        return [jnp.concatenate([a, b], axis=0) for a, b in zip(tabs, ident)]

    return jnp.stack(tables(HEAD_DIM) + tables(C_QK_DIM), axis=0)


def _lane_tile(vec, scale=1.0):
    return jnp.tile(vec.astype(F32) * scale, LANES // vec.shape[0])


def kernel(x, c, ctx, c_ctx, w_ada, b_ada, norm_mix, norm_ffn, w_in, w_out, a_q_gain, a_k_gain, d_q_gain, d_k_gain, conv_w, diff_lambda, diff_subln, dense_w_gate, dense_w_up, dense_w_down, router_w, router_b, moe_w_gate, moe_w_up, moe_w_down):
    b, seq, d = x.shape
    ctx_len = ctx.shape[1]
    depth = w_ada.shape[0]
    t = seq + ctx_len
    nt_all = t // TOK_TILE
    nt_lat = seq // TOK_TILE

    xa = jnp.concatenate([x, ctx], axis=1)
    rows = ((b + 1 + 7) // 8) * 8
    cond = jnp.concatenate([c, c_ctx[None, :], jnp.zeros((rows - b - 1, d), F32)], axis=0)
    mods = _ada_call(cond, w_ada, b_ada.reshape(depth, 1, 6 * d))
    rope = _rope_tables(seq, ctx_len)

    lane = jnp.arange(LANES)
    qmask = jnp.stack(
        [lane < 64, lane >= 64] + [(lane // 32) == g for g in range(4)]
        + [lane < 0, lane < 0], axis=0).astype(BF16)

    for l in range(depth):
        last = l == depth - 1
        lam_init = 0.8 - 0.6 * math.exp(-0.3 * l)
        mod = jnp.concatenate(
            [mods[l, :b].reshape(b, 1, 6, d),
             jnp.broadcast_to(mods[l, b].reshape(1, 1, 6, d), (b, 1, 6, d))], axis=1)
        zeros = jnp.zeros((LANES,), F32)
        gains = jnp.stack(
            [_lane_tile(a_q_gain[l], HEAD_DIM ** -0.5 * LOG2E), _lane_tile(a_k_gain[l]),
             _lane_tile(d_q_gain[l], C_QK_DIM ** -0.5 * LOG2E), _lane_tile(d_k_gain[l]),
             zeros, zeros, zeros, zeros], axis=0)
        q, kt, v, cv = _pre_attn_call(xa, mod, norm_mix[l][None, :], w_in[l].astype(BF16),
                                      gains, rope)
        zc = jnp.zeros((CONV_CH,), F32)
        misc = jnp.stack(
            [conv_w[l, 0], conv_w[l, 1], conv_w[l, 2],
             jnp.tile(diff_subln[l].astype(F32), CONV_CH // diff_subln.shape[1]),
             zc, zc, zc, zc], axis=0)
        n_tiles = nt_lat if last else nt_all
        xa = _attn_call(xa, q, kt, v, cv, mod, w_out[l].astype(BF16), qmask, misc,
                        diff_lambda[l].astype(F32), seq, lam_init, n_tiles)
        nw = norm_ffn[l][None, :]
        if l % 2 == 0:
            i = l // 2
            xa = _dense_ffn_call(xa, mod, nw, dense_w_gate[i].astype(BF16),
                                 dense_w_up[i].astype(BF16), dense_w_down[i].astype(BF16), nt_all)
        else:
            i = l // 2
            rw = jnp.pad(router_w[i], ((0, 0), (0, LANES - N_EXPERTS)))
            rb = jnp.pad(router_b[i], (0, LANES - N_EXPERTS))[None, :]
            h, comb = _router_call(xa, mod, nw, rw, rb, nt_all)
            for e in range(N_EXPERTS):
                xa = _expert_ffn_call(e, h, comb, xa, mod, moe_w_gate[i, e].astype(BF16),
                                      moe_w_up[i, e].astype(BF16),
                                      moe_w_down[i, e].astype(BF16), nt_all)
    return xa[:, :seq]
```

```python
import functools
import math

import jax
import jax.numpy as jnp
from jax import lax
from jax.experimental import pallas as pl
from jax.experimental.pallas import tpu as pltpu

F32 = jnp.float32
BF16 = jnp.bfloat16

GRID_W = 64
HEAD_DIM = 64
A_KV_HEADS = 2
CONV_CH = 256
C_HEADS = 4
C_QK_DIM = 32
N_EXPERTS = 8
ROPE_THETA = 10000.0
EPS = 1e-6
LOG2E = 1.4426950408889634

LANES = 128
TOK_TILE = 256
KEY_CHUNK = 256
KV_WIDTH = (A_KV_HEADS + C_HEADS // 2) * 2 * LANES
MAX_STATIC_BOUND = 60.0
VMEM_LIMIT = 56 * 1024 * 1024


def _cparams(n_axes):
    return pltpu.CompilerParams(
        dimension_semantics=("arbitrary",) * n_axes, vmem_limit_bytes=VMEM_LIMIT)


def _group_sum_matrix(group):
    r = lax.broadcasted_iota(jnp.int32, (LANES, LANES), 0)
    c = lax.broadcasted_iota(jnp.int32, (LANES, LANES), 1)
    shift = int(math.log2(group))
    same = (r >> shift) == (c >> shift)
    return jnp.where(same, 1.0, 0.0).astype(BF16)


def _group_rms(xc, gmat, group):
    ss = jnp.dot((xc * xc).astype(BF16), gmat, preferred_element_type=F32)
    return xc * lax.rsqrt(ss * (1.0 / group) + EPS)


def _ada_body(s_ref, w_ref, b_ref, o_ref):
    s = s_ref[...]
    s = s * (1.0 / (1.0 + jnp.exp(-s)))
    o_ref[...] = jnp.dot(s.astype(BF16), w_ref[...].astype(BF16),
                         preferred_element_type=F32) + b_ref[...]


def _ada_call(cond, w_ada, b_ada):
    depth, d, n = w_ada.shape
    r = cond.shape[0]
    tn = 1024
    return pl.pallas_call(
        _ada_body,
        grid=(depth, n // tn),
        in_specs=[
            pl.BlockSpec((r, d), lambda l, j: (0, 0)),
            pl.BlockSpec((None, d, tn), lambda l, j: (l, 0, j)),
            pl.BlockSpec((None, 1, tn), lambda l, j: (l, 0, j)),
        ],
        out_specs=pl.BlockSpec((None, r, tn), lambda l, j: (l, 0, j)),
        out_shape=jax.ShapeDtypeStruct((depth, r, n), F32),
        compiler_params=_cparams(2),
        name="ada_mod",
    )(cond, w_ada, b_ada)


def _pre_attn_body(x_ref, mod_ref, nw_ref, w_ref, gain_ref, rope_ref,
                   q_ref, kt_ref, v_ref, cv_ref):
    x = x_ref[...]
    h = x * lax.rsqrt(jnp.mean(x * x, axis=-1, keepdims=True) + EPS) * nw_ref[...]
    h = h * (1.0 + mod_ref[1:2, :]) + mod_ref[0:1, :]
    p = jnp.dot(h.astype(BF16), w_ref[...], preferred_element_type=F32)

    g64 = _group_sum_matrix(64)
    g32 = _group_sum_matrix(32)
    lane = lax.broadcasted_iota(jnp.int32, (x.shape[0], LANES), 1)

    def rope(xn, t0, half):
        return (xn * rope_ref[t0] + pltpu.roll(xn, LANES - half, 1) * rope_ref[t0 + 1]
                + pltpu.roll(xn, half, 1) * rope_ref[t0 + 2])

    def chunk(col):
        return p[:, col:col + LANES]

    for c in range(4):
        xn = _group_rms(chunk(c * LANES), g64, 64) * gain_ref[0:1, :]
        q_ref[:, c * LANES:(c + 1) * LANES] = rope(xn, 0, 16).astype(BF16)
    krow = lax.broadcasted_iota(jnp.int32, (LANES, x.shape[0]), 0)

    def bound_block(r):
        return jnp.where(krow == 0, gain_ref[r:r + 1, 0:1], 0.0).astype(BF16)

    ka = rope(_group_rms(chunk(512), g64, 64) * gain_ref[1:2, :], 0, 16)
    kat = ka.T.astype(BF16)
    for kv in range(A_KV_HEADS):
        r0 = kv * 2 * LANES
        kt_ref[r0:r0 + 64, :] = kat[kv * 64:(kv + 1) * 64]
        kt_ref[r0 + 64:r0 + LANES, :] = kat[kv * 64:(kv + 1) * 64]
        kt_ref[r0 + LANES:r0 + 2 * LANES, :] = bound_block(4)
    av = chunk(640)
    av_sw = pltpu.roll(av, 64, 1)
    v_ref[:, 0:LANES] = jnp.where(lane < 64, av, 1.0).astype(BF16)
    v_ref[:, LANES:2 * LANES] = jnp.where(lane < 64, 1.0, av_sw).astype(BF16)
    v_ref[:, 2 * LANES:3 * LANES] = jnp.where(lane < 64, av_sw, 1.0).astype(BF16)
    v_ref[:, 3 * LANES:4 * LANES] = jnp.where(lane < 64, 1.0, av).astype(BF16)
    cv_ref[:, 0:CONV_CH] = p[:, 1280:1536] * p[:, 768:1024]
    cv_ref[:, CONV_CH:2 * CONV_CH] = p[:, 1024:1280]
    for c in range(2):
        xn = _group_rms(chunk(1536 + c * LANES), g32, 32) * gain_ref[2:3, :]
        q_ref[:, 512 + c * LANES:512 + (c + 1) * LANES] = rope(xn, 3, 8).astype(BF16)
        kc = rope(_group_rms(chunk(1792 + c * LANES), g32, 32) * gain_ref[3:4, :], 3, 8)
        r0 = (A_KV_HEADS + c) * 2 * LANES
        kt_ref[r0:r0 + LANES, :] = kc.T.astype(BF16)
        kt_ref[r0 + LANES:r0 + 2 * LANES, :] = bound_block(5)
        dv = chunk(2048 + c * LANES)
        v_ref[:, r0:r0 + LANES] = jnp.where(lane < 64, dv, 1.0).astype(BF16)
        v_ref[:, r0 + LANES:r0 + 2 * LANES] = jnp.where(lane < 64, 1.0, dv).astype(BF16)


def _pre_attn_call(xa, mod, nw, w_in, gains, rope):
    b, t, d = xa.shape
    nt = t // TOK_TILE
    n_in = w_in.shape[1]
    return pl.pallas_call(
        _pre_attn_body,
        grid=(b, nt),
        in_specs=[
            pl.BlockSpec((None, TOK_TILE, d), lambda bi, i: (bi, i, 0)),
            pl.BlockSpec((None, None, 6, d), lambda bi, i: (bi, i // (nt - 1), 0, 0)),
            pl.BlockSpec((1, d), lambda bi, i: (0, 0)),
            pl.BlockSpec((d, n_in), lambda bi, i: (0, 0)),
            pl.BlockSpec((8, LANES), lambda bi, i: (0, 0)),
            pl.BlockSpec((6, TOK_TILE, LANES), lambda bi, i: (0, i, 0)),
        ],
        out_specs=[
            pl.BlockSpec((None, TOK_TILE, 768), lambda bi, i: (bi, i, 0)),
            pl.BlockSpec((None, KV_WIDTH, TOK_TILE), lambda bi, i: (bi, 0, i)),
            pl.BlockSpec((None, TOK_TILE, KV_WIDTH), lambda bi, i: (bi, i, 0)),
            pl.BlockSpec((None, TOK_TILE, 512), lambda bi, i: (bi, i, 0)),
        ],
        out_shape=[
            jax.ShapeDtypeStruct((b, t, 768), BF16),
            jax.ShapeDtypeStruct((b, KV_WIDTH, t), BF16),
            jax.ShapeDtypeStruct((b, t, KV_WIDTH), BF16),
            jax.ShapeDtypeStruct((b, t, 512), F32),
        ],
        compiler_params=_cparams(2),
        name="pre_attn",
    )(xa, mod, nw, w_in, gains, rope)


def _attn_body(seq, lam_init, static_bound, q_ref, kt_ref, v_ref, cv_ref, cvp_ref, cvn_ref, x_ref,
               mod_ref, wo_ref, qmask_ref, misc_ref, lamp_ref, o_ref):
    i = pl.program_id(1)
    t = kt_ref.shape[1]
    tq = q_ref.shape[0]
    n_lat = seq // tq
    lane = lax.broadcasted_iota(jnp.int32, (tq, LANES), 1)
    lamp = lamp_ref[...]
    lam = (jnp.exp(jnp.sum(lamp[0:1] * lamp[1:2], axis=1, keepdims=True))
           - jnp.exp(jnp.sum(lamp[2:3] * lamp[3:4], axis=1, keepdims=True)) + lam_init)
    unit = jnp.where(lane == 0, 1.0, 0.0).astype(BF16)

    def masked_q(col, mask_row):
        qc = q_ref[:, col:col + LANES] * qmask_ref[mask_row:mask_row + 1, :]
        return jnp.concatenate([qc, unit], axis=1)

    def attend(lhs_list, blk, k0, nk):
        r0 = blk * 2 * LANES
        if static_bound:
            lhs = jnp.concatenate(lhs_list, axis=0)
            acc = [None, None]
            for kc in range(k0, k0 + nk, KEY_CHUNK):
                s = jnp.dot(lhs, kt_ref[r0:r0 + 2 * LANES, kc:kc + KEY_CHUNK],
                            preferred_element_type=F32)
                e = jnp.exp2(s).astype(BF16)
                for half in range(2):
                    pv = jnp.dot(e[half * 2 * tq:(half + 1) * 2 * tq],
                                 v_ref[kc:kc + KEY_CHUNK, r0 + half * LANES:r0 + (half + 1) * LANES],
                                 preferred_element_type=F32)
                    acc[half] = pv if acc[half] is None else acc[half] + pv
            return acc
        outs = []
        for n, lhs in enumerate(lhs_list):
            s = jnp.dot(lhs, kt_ref[r0:r0 + 2 * LANES, k0:k0 + nk], preferred_element_type=F32)
            e = jnp.exp2(s - jnp.max(s, axis=1, keepdims=True)).astype(BF16)
            half = n // 2
            outs.append(jnp.dot(e, v_ref[k0:k0 + nk, r0 + half * LANES:r0 + (half + 1) * LANES],
                                preferred_element_type=F32))
        return [jnp.concatenate(outs[0:2], axis=0), jnp.concatenate(outs[2:4], axis=0)]

    def normalise(low, high):
        num = jnp.where(lane < 64, low, high)
        den = jnp.where(lane < 64, pltpu.roll(low, 64, 1), pltpu.roll(high, 64, 1))
        return num / den

    def mixers(k0, nk):
        ys = []
        for kv in range(A_KV_HEADS):
            cols = (2 * kv * LANES, (2 * kv + 1) * LANES)
            low, high = attend([masked_q(cols[0], 0), masked_q(cols[1], 0),
                                masked_q(cols[0], 1), masked_q(cols[1], 1)], kv, k0, nk)
            for n in range(2):
                ys.append(normalise(low[n * tq:(n + 1) * tq], high[n * tq:(n + 1) * tq]))
        u = cv_ref[:, 0:CONV_CH]
        row = lax.broadcasted_iota(jnp.int32, (tq, CONV_CH), 0)
        has_prev = jnp.where((i == 0) | (i == n_lat), 0.0, 1.0)
        has_next = jnp.where(i >= n_lat - 1, 0.0, 1.0)
        u_prev = jnp.where(row == 0, cvp_ref[7:8, 0:CONV_CH] * has_prev, pltpu.roll(u, 1, 0))
        u_next = jnp.where(row == tq - 1, cvn_ref[0:1, 0:CONV_CH] * has_next,
                           pltpu.roll(u, tq - 1, 0))
        conv = (misc_ref[0:1, :] * u_prev + misc_ref[1:2, :] * u + misc_ref[2:3, :] * u_next)
        b_out = cv_ref[:, CONV_CH:2 * CONV_CH] * conv
        ys.append(b_out[:, 0:LANES])
        ys.append(b_out[:, LANES:2 * LANES])
        g64 = _group_sum_matrix(64)
        for c in range(C_HEADS // 2):
            col = 512 + c * LANES
            low, high = attend([masked_q(col, 2 + g) for g in range(4)], A_KV_HEADS + c, k0, nk)
            yc = (normalise(low[0:tq], high[0:tq])
                  - lam * normalise(low[tq:2 * tq], high[tq:2 * tq]))
            ys.append(_group_rms(yc, g64, 64) * misc_ref[3:4, 0:LANES] * (1.0 - lam_init))
        y = jnp.concatenate(ys, axis=1).astype(BF16)
        o = jnp.dot(y, wo_ref[...], preferred_element_type=F32)
        o_ref[...] = x_ref[...] + mod_ref[2:3, :] * o

    @pl.when(i < n_lat)
    def _():
        mixers(0, t)

    @pl.when(i >= n_lat)
    def _():
        mixers(seq, t - seq)


def _attn_call(xa, q, kt, v, cv, mod, wo, qmask, misc, lamp, seq, lam_init, n_tiles, static_bound):
    b, t, d = xa.shape
    nt_all = t // TOK_TILE
    n8 = t // 8
    blk8 = TOK_TILE // 8
    return pl.pallas_call(
        functools.partial(_attn_body, seq, lam_init, static_bound),
        grid=(b, n_tiles),
        in_specs=[
            pl.BlockSpec((None, TOK_TILE, 768), lambda bi, i: (bi, i, 0)),
            pl.BlockSpec((None, KV_WIDTH, t), lambda bi, i: (bi, 0, 0)),
            pl.BlockSpec((None, t, KV_WIDTH), lambda bi, i: (bi, 0, 0)),
            pl.BlockSpec((None, TOK_TILE, 512), lambda bi, i: (bi, i, 0)),
            pl.BlockSpec((None, 8, 512), lambda bi, i: (bi, jnp.maximum(i * blk8 - 1, 0), 0)),
            pl.BlockSpec((None, 8, 512), lambda bi, i: (bi, jnp.minimum((i + 1) * blk8, n8 - 1), 0)),
            pl.BlockSpec((None, TOK_TILE, d), lambda bi, i: (bi, i, 0)),
            pl.BlockSpec((None, None, 6, d), lambda bi, i: (bi, i // (nt_all - 1), 0, 0)),
            pl.BlockSpec((d, d), lambda bi, i: (0, 0)),
            pl.BlockSpec((8, LANES), lambda bi, i: (0, 0)),
            pl.BlockSpec((8, CONV_CH), lambda bi, i: (0, 0)),
            pl.BlockSpec((4, C_QK_DIM), lambda bi, i: (0, 0)),
        ],
        out_specs=pl.BlockSpec((None, TOK_TILE, d), lambda bi, i: (bi, i, 0)),
        out_shape=jax.ShapeDtypeStruct((b, n_tiles * TOK_TILE, d), F32),
        compiler_params=_cparams(2),
        name="attn_merge",
    )(q, kt, v, cv, cv, cv, xa, mod, wo, qmask, misc, lamp)


def _mod_norm(x, mod_ref, nw_ref):
    h = x * lax.rsqrt(jnp.mean(x * x, axis=-1, keepdims=True) + EPS) * nw_ref[...]
    return h * (1.0 + mod_ref[4:5, :]) + mod_ref[3:4, :]


def _swiglu(h, wg_ref, wu_ref, wd_ref):
    g = jnp.dot(h, wg_ref[...], preferred_element_type=F32)
    u = jnp.dot(h, wu_ref[...], preferred_element_type=F32)
    a = g * (1.0 / (1.0 + jnp.exp(-g))) * u
    return jnp.dot(a.astype(BF16), wd_ref[...], preferred_element_type=F32)


def _dense_ffn_body(x_ref, mod_ref, nw_ref, wg_ref, wu_ref, wd_ref, o_ref):
    x = x_ref[...]
    h = _mod_norm(x, mod_ref, nw_ref).astype(BF16)
    o_ref[...] = x + mod_ref[5:6, :] * _swiglu(h, wg_ref, wu_ref, wd_ref)


def _dense_ffn_call(xa, mod, nw, wg, wu, wd, nt_all):
    b, t, d = xa.shape
    nt = t // TOK_TILE
    f = wg.shape[1]
    return pl.pallas_call(
        _dense_ffn_body,
        grid=(b, nt),
        in_specs=[
            pl.BlockSpec((None, TOK_TILE, d), lambda bi, i: (bi, i, 0)),
            pl.BlockSpec((None, None, 6, d), lambda bi, i: (bi, i // (nt_all - 1), 0, 0)),
            pl.BlockSpec((1, d), lambda bi, i: (0, 0)),
            pl.BlockSpec((d, f), lambda bi, i: (0, 0)),
            pl.BlockSpec((d, f), lambda bi, i: (0, 0)),
            pl.BlockSpec((f, d), lambda bi, i: (0, 0)),
        ],
        out_specs=pl.BlockSpec((None, TOK_TILE, d), lambda bi, i: (bi, i, 0)),
        out_shape=jax.ShapeDtypeStruct((b, t, d), F32),
        compiler_params=_cparams(2),
        name="dense_ffn",
    )(xa, mod, nw, wg, wu, wd)


def _router_body(x_ref, mod_ref, nw_ref, rw_ref, rb_ref, h_ref, route_ref, cnt_ref, carry_ref):
    @pl.when((pl.program_id(0) == 0) & (pl.program_id(1) == 0))
    def _():
        carry_ref[...] = jnp.zeros_like(carry_ref)

    h = _mod_norm(x_ref[...], mod_ref, nw_ref)
    h_hi = h.astype(BF16)
    h_lo = (h - h_hi.astype(F32)).astype(BF16)
    rw = rw_ref[...]
    rw_hi = rw.astype(BF16)
    rw_lo = (rw - rw_hi.astype(F32)).astype(BF16)
    logits = (jnp.dot(h_hi, rw_hi, preferred_element_type=F32)
              + (jnp.dot(h_hi, rw_lo, preferred_element_type=F32)
                 + jnp.dot(h_lo, rw_hi, preferred_element_type=F32))) + rb_ref[...]
    idx = lax.broadcasted_iota(jnp.int32, logits.shape, 1).astype(F32)
    neg = jnp.float32(-jnp.inf)
    logits = jnp.where(idx < N_EXPERTS, logits, neg)
    m1 = jnp.max(logits, axis=1, keepdims=True)
    i1 = jnp.min(jnp.where(logits == m1, idx, float(LANES)), axis=1, keepdims=True)
    rest = jnp.where(idx == i1, neg, logits)
    m2 = jnp.max(rest, axis=1, keepdims=True)
    i2 = jnp.min(jnp.where(rest == m2, idx, float(LANES)), axis=1, keepdims=True)
    e2 = jnp.exp(m2 - m1)
    den = 1.0 / (1.0 + e2)
    oh1 = jnp.where(idx == i1, 1.0, 0.0)
    oh2 = jnp.where(idx == i2, 1.0, 0.0)
    oh = oh1 + oh2
    n = h.shape[0]
    r = lax.broadcasted_iota(jnp.int32, (n, n), 0)
    c = lax.broadcasted_iota(jnp.int32, (n, n), 1)
    before = jnp.where(c < r, 1.0, 0.0).astype(BF16)
    cum = jnp.dot(before, oh.astype(BF16), preferred_element_type=F32) + carry_ref[0:1, :]
    rank1 = jnp.sum(cum * oh1, axis=1, keepdims=True)
    rank2 = jnp.sum(cum * oh2, axis=1, keepdims=True)
    carry_ref[...] = carry_ref[...] + jnp.sum(oh, axis=0, keepdims=True)
    cnt_ref[...] = carry_ref[...]
    vals = (i1, i2, rank1, rank2, den, e2 * den)
    route = jnp.zeros_like(logits)
    for k, val in enumerate(vals):
        route = jnp.where(idx == k, val, route)
    route_ref[...] = route
    for s in range(h.shape[1] // LANES):
        h_ref[:, s, :] = h[:, s * LANES:(s + 1) * LANES]


def _router_call(xa, mod, nw, rw, rb, nt_all):
    b, t, d = xa.shape
    nt = t // TOK_TILE
    return pl.pallas_call(
        _router_body,
        grid=(b, nt),
        in_specs=[
            pl.BlockSpec((None, TOK_TILE, d), lambda bi, i: (bi, i, 0)),
            pl.BlockSpec((None, None, 6, d), lambda bi, i: (bi, i // (nt_all - 1), 0, 0)),
            pl.BlockSpec((1, d), lambda bi, i: (0, 0)),
            pl.BlockSpec((d, LANES), lambda bi, i: (0, 0)),
            pl.BlockSpec((1, LANES), lambda bi, i: (0, 0)),
        ],
        out_specs=[
            pl.BlockSpec((TOK_TILE, d // LANES, LANES), lambda bi, i: (bi * nt + i, 0, 0)),
            pl.BlockSpec((None, TOK_TILE, LANES), lambda bi, i: (bi, i, 0)),
            pl.BlockSpec((8, LANES), lambda bi, i: (0, 0)),
        ],
        out_shape=[
            jax.ShapeDtypeStruct((b * t, d // LANES, LANES), F32),
            jax.ShapeDtypeStruct((b, t, LANES), F32),
            jax.ShapeDtypeStruct((8, LANES), F32),
        ],
        scratch_shapes=[pltpu.VMEM((8, LANES), F32)],
        compiler_params=_cparams(2),
        name="router",
    )(xa, mod, nw, rw, rb)


def _row_copy_wait(src_ref, dst_ref, sem, count):
    def body(_, carry):
        pltpu.make_async_copy(src_ref.at[0], dst_ref.at[0], sem).wait()
        return carry
    lax.fori_loop(0, count, body, 0, unroll=8)


def _dispatch_body(dest_ref, h_ref, xs_in_ref, xs_ref, sem):
    del xs_in_ref
    n = h_ref.shape[0]

    def issue(r, carry):
        for k in range(2):
            pltpu.make_async_copy(h_ref.at[r], xs_ref.at[dest_ref[0, k * n + r]], sem).start()
        return carry
    lax.fori_loop(0, n, issue, 0, unroll=8)
    _row_copy_wait(h_ref, xs_ref, sem, 2 * n)


def _dispatch_call(dest, h, xs_init):
    n_tok, s, _ = h.shape
    nt = n_tok // TOK_TILE
    return pl.pallas_call(
        _dispatch_body,
        grid=(nt,),
        in_specs=[
            pl.BlockSpec((None, 1, 2 * TOK_TILE), lambda i: (i, 0, 0), memory_space=pltpu.SMEM),
            pl.BlockSpec((TOK_TILE, s, LANES), lambda i: (i, 0, 0)),
            pl.BlockSpec(memory_space=pl.ANY),
        ],
        out_specs=pl.BlockSpec(memory_space=pl.ANY),
        out_shape=jax.ShapeDtypeStruct(xs_init.shape, F32),
        scratch_shapes=[pltpu.SemaphoreType.DMA(())],
        input_output_aliases={2: 0},
        compiler_params=_cparams(1),
        name="moe_dispatch",
    )(dest, h, xs_init)


def _grouped_ffn_body(te_ref, nu_ref, xs_ref, wg_ref, wu_ref, wd_ref, ys_ref):
    del te_ref
    used = pl.program_id(0) < nu_ref[0]

    @pl.when(used)
    def _():
        s = xs_ref.shape[1]
        h = jnp.concatenate([xs_ref[:, k, :] for k in range(s)], axis=1).astype(BF16)
        y = _swiglu(h, wg_ref, wu_ref, wd_ref)
        for k in range(s):
            ys_ref[:, k, :] = y[:, k * LANES:(k + 1) * LANES]

    @pl.when(jnp.logical_not(used))
    def _():
        ys_ref[...] = jnp.zeros_like(ys_ref)


def _grouped_ffn_call(tile_expert, n_used, xs, wg, wu, wd):
    n_slots, s, _ = xs.shape
    n_tiles = n_slots // TOK_TILE
    _, d, f = wg.shape

    return pl.pallas_call(
        _grouped_ffn_body,
        grid_spec=pltpu.PrefetchScalarGridSpec(
            num_scalar_prefetch=2,
            grid=(n_tiles,),
            in_specs=[
                pl.BlockSpec((TOK_TILE, s, LANES),
                             lambda j, te, nu: (jnp.minimum(j, nu[0] - 1), 0, 0)),
                pl.BlockSpec((None, d, f), lambda j, te, nu: (te[j], 0, 0)),
                pl.BlockSpec((None, d, f), lambda j, te, nu: (te[j], 0, 0)),
                pl.BlockSpec((None, f, d), lambda j, te, nu: (te[j], 0, 0)),
            ],
            out_specs=pl.BlockSpec((TOK_TILE, s, LANES), lambda j, te, nu: (j, 0, 0)),
        ),
        out_shape=jax.ShapeDtypeStruct(xs.shape, F32),
        compiler_params=_cparams(1),
        name="grouped_ffn",
    )(tile_expert, n_used, xs, wg, wu, wd)


def _combine_body(dest_ref, x_ref, route_ref, mod_ref, ys_ref, o_ref, buf_ref, sem):
    n = x_ref.shape[0]

    def issue(r, carry):
        for k in range(2):
            pltpu.make_async_copy(ys_ref.at[dest_ref[0, k * n + r]], buf_ref.at[k, r], sem).start()
        return carry
    lax.fori_loop(0, n, issue, 0, unroll=8)
    _row_copy_wait(ys_ref, buf_ref.at[0], sem, 2 * n)
    gate1 = route_ref[:, 4:5]
    gate2 = route_ref[:, 5:6]
    for s in range(buf_ref.shape[2]):
        cols = slice(s * LANES, (s + 1) * LANES)
        y = gate1 * buf_ref[0, :, s, :] + gate2 * buf_ref[1, :, s, :]
        o_ref[:, cols] = x_ref[:, cols] + mod_ref[5:6, cols] * y


def _combine_call(dest, xa, route, mod, ys, nt_all):
    b, t, d = xa.shape
    nt = t // TOK_TILE
    s = d // LANES
    return pl.pallas_call(
        _combine_body,
        grid=(b, nt),
        in_specs=[
            pl.BlockSpec((None, 1, 2 * TOK_TILE), lambda bi, i: (bi * nt + i, 0, 0),
                         memory_space=pltpu.SMEM),
            pl.BlockSpec((None, TOK_TILE, d), lambda bi, i: (bi, i, 0)),
            pl.BlockSpec((None, TOK_TILE, LANES), lambda bi, i: (bi, i, 0)),
            pl.BlockSpec((None, None, 6, d), lambda bi, i: (bi, i // (nt_all - 1), 0, 0)),
            pl.BlockSpec(memory_space=pl.ANY),
        ],
        out_specs=pl.BlockSpec((None, TOK_TILE, d), lambda bi, i: (bi, i, 0)),
        out_shape=jax.ShapeDtypeStruct((b, t, d), F32),
        scratch_shapes=[pltpu.VMEM((2, TOK_TILE, s, LANES), F32), pltpu.SemaphoreType.DMA(())],
        compiler_params=_cparams(2),
        name="moe_combine",
    )(dest, xa, route, mod, ys)


def _moe_layer(xa, mod, nw, rw, rb, wg, wu, wd, nt_all):
    b, t, d = xa.shape
    n_tok = b * t
    n_exp = wg.shape[0]
    h, route, counts = _router_call(xa, mod, nw, rw, rb, nt_all)
    cnt = counts[0, :n_exp].astype(jnp.int32)
    padded = ((cnt + TOK_TILE - 1) // TOK_TILE) * TOK_TILE
    ends = jnp.cumsum(padded)
    starts = ends - padded
    n_tiles = (2 * n_tok) // TOK_TILE + n_exp
    n_used = (ends[-1] // TOK_TILE).astype(jnp.int32)
    tile_start = jnp.arange(n_tiles, dtype=jnp.int32) * TOK_TILE
    tile_expert = jnp.minimum(jnp.searchsorted(ends, tile_start, side="right"), n_exp - 1)
    tile_expert = tile_expert.astype(jnp.int32)
    expert = route[..., 0:2].astype(jnp.int32)
    rank = route[..., 2:4].astype(jnp.int32)
    dest = starts[expert] + rank
    dest = dest.reshape(b * (t // TOK_TILE), TOK_TILE, 2).transpose(0, 2, 1)
    dest = dest.reshape(b * (t // TOK_TILE), 1, 2 * TOK_TILE)
    xs = _dispatch_call(dest, h, jnp.zeros((n_tiles * TOK_TILE, d // LANES, LANES), F32))
    ys = _grouped_ffn_call(tile_expert, n_used.reshape(1), xs, wg, wu, wd)
    return _combine_call(dest, xa, route, mod, ys, nt_all)


def _rope_tables(seq, ctx_len):
    pos_row = jnp.arange(seq, dtype=jnp.int32) // GRID_W
    pos_col = jnp.arange(seq, dtype=jnp.int32) % GRID_W

    def tables(width):
        n = width // 2
        half = n // 2
        inv = ROPE_THETA ** (-jnp.arange(0, n, 2, dtype=F32) / n)
        d = jnp.arange(LANES)
        r = d % n
        first = (r < half)[None, :]
        is_col = ((d % width) // n == 1)[None, :]
        pos = jnp.where(is_col, pos_col[:, None], pos_row[:, None]).astype(F32)
        ang = pos * inv[r % half][None, :]
        cos, sin = jnp.cos(ang), jnp.sin(ang)
        tabs = [cos, jnp.where(first, -sin, 0.0), jnp.where(first, 0.0, sin)]
        ident = [jnp.ones((ctx_len, LANES), F32), jnp.zeros((ctx_len, LANES), F32),
                 jnp.zeros((ctx_len, LANES), F32)]
        return [jnp.concatenate([a, b], axis=0) for a, b in zip(tabs, ident)]

    return jnp.stack(tables(HEAD_DIM) + tables(C_QK_DIM), axis=0)


def _lane_tile(vec, scale=1.0):
    return jnp.tile(vec.astype(F32) * scale, LANES // vec.shape[0])


def _score_bound(q_gain, k_gain, dim):
    g = jnp.max(jnp.abs(q_gain.astype(F32))) * jnp.max(jnp.abs(k_gain.astype(F32)))
    return jnp.ceil(math.sqrt(dim) * LOG2E * 1.01 * g) + 1.0


def kernel(x, c, ctx, c_ctx, w_ada, b_ada, norm_mix, norm_ffn, w_in, w_out, a_q_gain, a_k_gain, d_q_gain, d_k_gain, conv_w, diff_lambda, diff_subln, dense_w_gate, dense_w_up, dense_w_down, router_w, router_b, moe_w_gate, moe_w_up, moe_w_down):
    b, seq, d = x.shape
    ctx_len = ctx.shape[1]
    depth = w_ada.shape[0]
    t = seq + ctx_len
    nt_all = t // TOK_TILE
    nt_lat = seq // TOK_TILE

    xa = jnp.concatenate([x, ctx], axis=1)
    rows = ((b + 1 + 7) // 8) * 8
    cond = jnp.concatenate([c, c_ctx[None, :], jnp.zeros((rows - b - 1, d), F32)], axis=0)
    mods = _ada_call(cond, w_ada, b_ada.reshape(depth, 1, 6 * d))
    rope = _rope_tables(seq, ctx_len)

    lane = jnp.arange(LANES)
    qmask = jnp.stack(
        [lane < 64, lane >= 64] + [(lane // 32) == g for g in range(4)]
        + [lane < 0, lane < 0], axis=0).astype(BF16)

    for l in range(depth):
        last = l == depth - 1
        lam_init = 0.8 - 0.6 * math.exp(-0.3 * l)
        mod = jnp.concatenate(
            [mods[l, :b].reshape(b, 1, 6, d),
             jnp.broadcast_to(mods[l, b].reshape(1, 1, 6, d), (b, 1, 6, d))], axis=1)
        zeros = jnp.zeros((LANES,), F32)
        bound_a = _score_bound(a_q_gain[l], a_k_gain[l], HEAD_DIM)
        bound_c = _score_bound(d_q_gain[l], d_k_gain[l], C_QK_DIM)
        static_ok = jnp.maximum(bound_a, bound_c) <= MAX_STATIC_BOUND
        gains = jnp.stack(
            [_lane_tile(a_q_gain[l], HEAD_DIM ** -0.5 * LOG2E), _lane_tile(a_k_gain[l]),
             _lane_tile(d_q_gain[l], C_QK_DIM ** -0.5 * LOG2E), _lane_tile(d_k_gain[l]),
             zeros - jnp.where(static_ok, bound_a, 0.0), zeros - jnp.where(static_ok, bound_c, 0.0),
             zeros, zeros], axis=0)
        q, kt, v, cv = _pre_attn_call(xa, mod, norm_mix[l][None, :], w_in[l].astype(BF16),
                                      gains, rope)
        zc = jnp.zeros((CONV_CH,), F32)
        misc = jnp.stack(
            [conv_w[l, 0], conv_w[l, 1], conv_w[l, 2],
             jnp.tile(diff_subln[l].astype(F32), CONV_CH // diff_subln.shape[1]),
             zc, zc, zc, zc], axis=0)
        n_tiles = nt_lat if last else nt_all
        attn_args = (xa, q, kt, v, cv, mod, w_out[l].astype(BF16), qmask, misc,
                     diff_lambda[l].astype(F32))
        xa = lax.cond(
            static_ok,
            lambda *a: _attn_call(*a, seq, lam_init, n_tiles, True),
            lambda *a: _attn_call(*a, seq, lam_init, n_tiles, False),
            *attn_args)
        nw = norm_ffn[l][None, :]
        if l % 2 == 0:
            i = l // 2
            xa = _dense_ffn_call(xa, mod, nw, dense_w_gate[i].astype(BF16),
                                 dense_w_up[i].astype(BF16), dense_w_down[i].astype(BF16), nt_all)
        else:
            i = l // 2
            rw = jnp.pad(router_w[i], ((0, 0), (0, LANES - N_EXPERTS)))
            rb = jnp.pad(router_b[i], (0, LANES - N_EXPERTS))[None, :]
            xa = _moe_layer(xa, mod, nw, rw, rb, moe_w_gate[i].astype(BF16),
                            moe_w_up[i].astype(BF16), moe_w_down[i].astype(BF16), nt_all)
    return xa[:, :seq]
```

```python
import functools
import math

import jax
import jax.numpy as jnp
from jax import lax
from jax.experimental import pallas as pl
from jax.experimental.pallas import tpu as pltpu

F32 = jnp.float32
BF16 = jnp.bfloat16

GRID_W = 64
HEAD_DIM = 64
A_KV_HEADS = 2
CONV_CH = 256
C_HEADS = 4
C_QK_DIM = 32
N_EXPERTS = 8
ROPE_THETA = 10000.0
EPS = 1e-6
LOG2E = 1.4426950408889634

LANES = 128
TOK_TILE = 256
KEY_CHUNK = 256
KV_WIDTH = (A_KV_HEADS + C_HEADS // 2) * 2 * LANES
MAX_STATIC_BOUND = 60.0
VMEM_LIMIT = 56 * 1024 * 1024


def _cparams(n_axes):
    return pltpu.CompilerParams(
        dimension_semantics=("arbitrary",) * n_axes, vmem_limit_bytes=VMEM_LIMIT)


def _group_sum_matrix(group):
    r = lax.broadcasted_iota(jnp.int32, (LANES, LANES), 0)
    c = lax.broadcasted_iota(jnp.int32, (LANES, LANES), 1)
    shift = int(math.log2(group))
    same = (r >> shift) == (c >> shift)
    return jnp.where(same, 1.0, 0.0).astype(BF16)


def _group_rms(xc, gmat, group):
    ss = jnp.dot((xc * xc).astype(BF16), gmat, preferred_element_type=F32)
    return xc * lax.rsqrt(ss * (1.0 / group) + EPS)


def _ada_body(s_ref, w_ref, b_ref, o_ref):
    s = s_ref[...]
    s = s * (1.0 / (1.0 + jnp.exp(-s)))
    o_ref[...] = jnp.dot(s.astype(BF16), w_ref[...].astype(BF16),
                         preferred_element_type=F32) + b_ref[...]


def _ada_call(cond, w_ada, b_ada):
    depth, d, n = w_ada.shape
    r = cond.shape[0]
    tn = 1024
    return pl.pallas_call(
        _ada_body,
        grid=(depth, n // tn),
        in_specs=[
            pl.BlockSpec((r, d), lambda l, j: (0, 0)),
            pl.BlockSpec((None, d, tn), lambda l, j: (l, 0, j)),
            pl.BlockSpec((None, 1, tn), lambda l, j: (l, 0, j)),
        ],
        out_specs=pl.BlockSpec((None, r, tn), lambda l, j: (l, 0, j)),
        out_shape=jax.ShapeDtypeStruct((depth, r, n), F32),
        compiler_params=_cparams(2),
        name="ada_mod",
    )(cond, w_ada, b_ada)


def _pre_attn_body(x_ref, mod_ref, nw_ref, w_ref, gain_ref, rope_ref,
                   q_ref, kt_ref, v_ref, cv_ref):
    x = x_ref[...]
    h = x * lax.rsqrt(jnp.mean(x * x, axis=-1, keepdims=True) + EPS) * nw_ref[...]
    h = h * (1.0 + mod_ref[1:2, :]) + mod_ref[0:1, :]
    p = jnp.dot(h.astype(BF16), w_ref[...], preferred_element_type=F32)

    g64 = _group_sum_matrix(64)
    g32 = _group_sum_matrix(32)
    lane = lax.broadcasted_iota(jnp.int32, (x.shape[0], LANES), 1)

    def rope(xn, t0, half):
        return (xn * rope_ref[t0] + pltpu.roll(xn, LANES - half, 1) * rope_ref[t0 + 1]
                + pltpu.roll(xn, half, 1) * rope_ref[t0 + 2])

    def chunk(col):
        return p[:, col:col + LANES]

    for c in range(4):
        xn = _group_rms(chunk(c * LANES), g64, 64) * gain_ref[0:1, :]
        q_ref[:, c * LANES:(c + 1) * LANES] = rope(xn, 0, 16).astype(BF16)
    krow = lax.broadcasted_iota(jnp.int32, (LANES, x.shape[0]), 0)

    def bound_block(r):
        return jnp.where(krow == 0, gain_ref[r:r + 1, 0:1], 0.0).astype(BF16)

    ka = rope(_group_rms(chunk(512), g64, 64) * gain_ref[1:2, :], 0, 16)
    kat = ka.T.astype(BF16)
    for kv in range(A_KV_HEADS):
        r0 = kv * 2 * LANES
        kt_ref[r0:r0 + 64, :] = kat[kv * 64:(kv + 1) * 64]
        kt_ref[r0 + 64:r0 + LANES, :] = kat[kv * 64:(kv + 1) * 64]
        kt_ref[r0 + LANES:r0 + 2 * LANES, :] = bound_block(4)
    av = chunk(640)
    av_sw = pltpu.roll(av, 64, 1)
    v_ref[:, 0:LANES] = jnp.where(lane < 64, av, 1.0).astype(BF16)
    v_ref[:, LANES:2 * LANES] = jnp.where(lane < 64, 1.0, av_sw).astype(BF16)
    v_ref[:, 2 * LANES:3 * LANES] = jnp.where(lane < 64, av_sw, 1.0).astype(BF16)
    v_ref[:, 3 * LANES:4 * LANES] = jnp.where(lane < 64, 1.0, av).astype(BF16)
    cv_ref[:, 0:CONV_CH] = p[:, 1280:1536] * p[:, 768:1024]
    cv_ref[:, CONV_CH:2 * CONV_CH] = p[:, 1024:1280]
    for c in range(2):
        xn = _group_rms(chunk(1536 + c * LANES), g32, 32) * gain_ref[2:3, :]
        q_ref[:, 512 + c * LANES:512 + (c + 1) * LANES] = rope(xn, 3, 8).astype(BF16)
        kc = rope(_group_rms(chunk(1792 + c * LANES), g32, 32) * gain_ref[3:4, :], 3, 8)
        r0 = (A_KV_HEADS + c) * 2 * LANES
        kt_ref[r0:r0 + LANES, :] = kc.T.astype(BF16)
        kt_ref[r0 + LANES:r0 + 2 * LANES, :] = bound_block(5)
        dv = chunk(2048 + c * LANES)
        v_ref[:, r0:r0 + LANES] = jnp.where(lane < 64, dv, 1.0).astype(BF16)
        v_ref[:, r0 + LANES:r0 + 2 * LANES] = jnp.where(lane < 64, 1.0, dv).astype(BF16)


def _pre_attn_call(xa, mod, nw, w_in, gains, rope):
    b, t, d = xa.shape
    nt = t // TOK_TILE
    n_in = w_in.shape[1]
    return pl.pallas_call(
        _pre_attn_body,
        grid=(b, nt),
        in_specs=[
            pl.BlockSpec((None, TOK_TILE, d), lambda bi, i: (bi, i, 0)),
            pl.BlockSpec((None, None, 6, d), lambda bi, i: (bi, i // (nt - 1), 0, 0)),
            pl.BlockSpec((1, d), lambda bi, i: (0, 0)),
            pl.BlockSpec((d, n_in), lambda bi, i: (0, 0)),
            pl.BlockSpec((8, LANES), lambda bi, i: (0, 0)),
            pl.BlockSpec((6, TOK_TILE, LANES), lambda bi, i: (0, i, 0)),
        ],
        out_specs=[
            pl.BlockSpec((None, TOK_TILE, 768), lambda bi, i: (bi, i, 0)),
            pl.BlockSpec((None, KV_WIDTH, TOK_TILE), lambda bi, i: (bi, 0, i)),
            pl.BlockSpec((None, TOK_TILE, KV_WIDTH), lambda bi, i: (bi, i, 0)),
            pl.BlockSpec((None, TOK_TILE, 512), lambda bi, i: (bi, i, 0)),
        ],
        out_shape=[
            jax.ShapeDtypeStruct((b, t, 768), BF16),
            jax.ShapeDtypeStruct((b, KV_WIDTH, t), BF16),
            jax.ShapeDtypeStruct((b, t, KV_WIDTH), BF16),
            jax.ShapeDtypeStruct((b, t, 512), F32),
        ],
        compiler_params=_cparams(2),
        name="pre_attn",
    )(xa, mod, nw, w_in, gains, rope)


def _attn_body(seq, lam_init, static_bound, q_ref, kt_ref, v_ref, cv_ref, cvp_ref, cvn_ref, x_ref,
               mod_ref, wo_ref, qmask_ref, misc_ref, lamp_ref, o_ref):
    i = pl.program_id(1)
    t = kt_ref.shape[1]
    tq = q_ref.shape[0]
    n_lat = seq // tq
    lane = lax.broadcasted_iota(jnp.int32, (tq, LANES), 1)
    lamp = lamp_ref[...]
    lam = (jnp.exp(jnp.sum(lamp[0:1] * lamp[1:2], axis=1, keepdims=True))
           - jnp.exp(jnp.sum(lamp[2:3] * lamp[3:4], axis=1, keepdims=True)) + lam_init)
    unit = jnp.where(lane == 0, 1.0, 0.0).astype(BF16)

    def masked_q(col, mask_row):
        qc = q_ref[:, col:col + LANES] * qmask_ref[mask_row:mask_row + 1, :]
        return jnp.concatenate([qc, unit], axis=1)

    def attend_pair(lhs_pair, blk, half, k0, nk):
        r0 = blk * 2 * LANES
        vc = r0 + half * LANES
        if static_bound:
            lhs = jnp.concatenate(lhs_pair, axis=0)
            acc = None
            for kc in range(k0, k0 + nk, KEY_CHUNK):
                s = jnp.dot(lhs, kt_ref[r0:r0 + 2 * LANES, kc:kc + KEY_CHUNK],
                            preferred_element_type=F32)
                pv = jnp.dot(jnp.exp2(s).astype(BF16), v_ref[kc:kc + KEY_CHUNK, vc:vc + LANES],
                             preferred_element_type=F32)
                acc = pv if acc is None else acc + pv
            return acc
        outs = []
        for lhs in lhs_pair:
            s = jnp.dot(lhs, kt_ref[r0:r0 + 2 * LANES, k0:k0 + nk], preferred_element_type=F32)
            e = jnp.exp2(s - jnp.max(s, axis=1, keepdims=True)).astype(BF16)
            outs.append(jnp.dot(e, v_ref[k0:k0 + nk, vc:vc + LANES], preferred_element_type=F32))
        return jnp.concatenate(outs, axis=0)

    def attend(lhs_list, blk, k0, nk):
        return (attend_pair(lhs_list[0:2], blk, 0, k0, nk),
                attend_pair(lhs_list[2:4], blk, 1, k0, nk))

    def normalise(low, high):
        num = jnp.where(lane < 64, low, high)
        den = jnp.where(lane < 64, pltpu.roll(low, 64, 1), pltpu.roll(high, 64, 1))
        return num / den

    def mixers(k0, nk):
        ys = []
        for kv in range(A_KV_HEADS):
            cols = (2 * kv * LANES, (2 * kv + 1) * LANES)
            low, high = attend([masked_q(cols[0], 0), masked_q(cols[1], 0),
                                masked_q(cols[0], 1), masked_q(cols[1], 1)], kv, k0, nk)
            for n in range(2):
                ys.append(normalise(low[n * tq:(n + 1) * tq], high[n * tq:(n + 1) * tq]))
        u = cv_ref[:, 0:CONV_CH]
        row = lax.broadcasted_iota(jnp.int32, (tq, CONV_CH), 0)
        has_prev = jnp.where((i == 0) | (i == n_lat), 0.0, 1.0)
        has_next = jnp.where(i >= n_lat - 1, 0.0, 1.0)
        u_prev = jnp.where(row == 0, cvp_ref[7:8, 0:CONV_CH] * has_prev, pltpu.roll(u, 1, 0))
        u_next = jnp.where(row == tq - 1, cvn_ref[0:1, 0:CONV_CH] * has_next,
                           pltpu.roll(u, tq - 1, 0))
        conv = (misc_ref[0:1, :] * u_prev + misc_ref[1:2, :] * u + misc_ref[2:3, :] * u_next)
        b_out = cv_ref[:, CONV_CH:2 * CONV_CH] * conv
        ys.append(b_out[:, 0:LANES])
        ys.append(b_out[:, LANES:2 * LANES])
        g64 = _group_sum_matrix(64)
        for c in range(C_HEADS // 2):
            col = 512 + c * LANES
            low, high = attend([masked_q(col, 2 + g) for g in range(4)], A_KV_HEADS + c, k0, nk)
            yc = (normalise(low[0:tq], high[0:tq])
                  - lam * normalise(low[tq:2 * tq], high[tq:2 * tq]))
            ys.append(_group_rms(yc, g64, 64) * misc_ref[3:4, 0:LANES] * (1.0 - lam_init))
        y = jnp.concatenate(ys, axis=1).astype(BF16)
        o = jnp.dot(y, wo_ref[...], preferred_element_type=F32)
        o_ref[...] = x_ref[...] + mod_ref[2:3, :] * o

    @pl.when(i < n_lat)
    def _():
        mixers(0, t)

    @pl.when(i >= n_lat)
    def _():
        mixers(seq, t - seq)


def _attn_call(xa, q, kt, v, cv, mod, wo, qmask, misc, lamp, seq, lam_init, n_tiles, static_bound):
    b, t, d = xa.shape
    nt_all = t // TOK_TILE
    n8 = t // 8
    blk8 = TOK_TILE // 8
    return pl.pallas_call(
        functools.partial(_attn_body, seq, lam_init, static_bound),
        grid=(b, n_tiles),
        in_specs=[
            pl.BlockSpec((None, TOK_TILE, 768), lambda bi, i: (bi, i, 0)),
            pl.BlockSpec((None, KV_WIDTH, t), lambda bi, i: (bi, 0, 0)),
            pl.BlockSpec((None, t, KV_WIDTH), lambda bi, i: (bi, 0, 0)),
            pl.BlockSpec((None, TOK_TILE, 512), lambda bi, i: (bi, i, 0)),
            pl.BlockSpec((None, 8, 512), lambda bi, i: (bi, jnp.maximum(i * blk8 - 1, 0), 0)),
            pl.BlockSpec((None, 8, 512), lambda bi, i: (bi, jnp.minimum((i + 1) * blk8, n8 - 1), 0)),
            pl.BlockSpec((None, TOK_TILE, d), lambda bi, i: (bi, i, 0)),
            pl.BlockSpec((None, None, 6, d), lambda bi, i: (bi, i // (nt_all - 1), 0, 0)),
            pl.BlockSpec((d, d), lambda bi, i: (0, 0)),
            pl.BlockSpec((8, LANES), lambda bi, i: (0, 0)),
            pl.BlockSpec((8, CONV_CH), lambda bi, i: (0, 0)),
            pl.BlockSpec((4, C_QK_DIM), lambda bi, i: (0, 0)),
        ],
        out_specs=pl.BlockSpec((None, TOK_TILE, d), lambda bi, i: (bi, i, 0)),
        out_shape=jax.ShapeDtypeStruct((b, n_tiles * TOK_TILE, d), F32),
        compiler_params=_cparams(2),
        name="attn_merge",
    )(q, kt, v, cv, cv, cv, xa, mod, wo, qmask, misc, lamp)


def _mod_norm(x, mod_ref, nw_ref):
    h = x * lax.rsqrt(jnp.mean(x * x, axis=-1, keepdims=True) + EPS) * nw_ref[...]
    return h * (1.0 + mod_ref[4:5, :]) + mod_ref[3:4, :]


def _swiglu(h, wg_ref, wu_ref, wd_ref):
    g = jnp.dot(h, wg_ref[...], preferred_element_type=F32)
    u = jnp.dot(h, wu_ref[...], preferred_element_type=F32)
    a = g * (1.0 / (1.0 + jnp.exp(-g))) * u
    return jnp.dot(a.astype(BF16), wd_ref[...], preferred_element_type=F32)


def _dense_ffn_body(x_ref, mod_ref, nw_ref, wg_ref, wu_ref, wd_ref, o_ref):
    x = x_ref[...]
    h = _mod_norm(x, mod_ref, nw_ref).astype(BF16)
    o_ref[...] = x + mod_ref[5:6, :] * _swiglu(h, wg_ref, wu_ref, wd_ref)


def _dense_ffn_call(xa, mod, nw, wg, wu, wd, nt_all):
    b, t, d = xa.shape
    nt = t // TOK_TILE
    f = wg.shape[1]
    return pl.pallas_call(
        _dense_ffn_body,
        grid=(b, nt),
        in_specs=[
            pl.BlockSpec((None, TOK_TILE, d), lambda bi, i: (bi, i, 0)),
            pl.BlockSpec((None, None, 6, d), lambda bi, i: (bi, i // (nt_all - 1), 0, 0)),
            pl.BlockSpec((1, d), lambda bi, i: (0, 0)),
            pl.BlockSpec((d, f), lambda bi, i: (0, 0)),
            pl.BlockSpec((d, f), lambda bi, i: (0, 0)),
            pl.BlockSpec((f, d), lambda bi, i: (0, 0)),
        ],
        out_specs=pl.BlockSpec((None, TOK_TILE, d), lambda bi, i: (bi, i, 0)),
        out_shape=jax.ShapeDtypeStruct((b, t, d), F32),
        compiler_params=_cparams(2),
        name="dense_ffn",
    )(xa, mod, nw, wg, wu, wd)


def _router_body(x_ref, mod_ref, nw_ref, rw_ref, rb_ref, h_ref, route_ref, cnt_ref, carry_ref):
    @pl.when((pl.program_id(0) == 0) & (pl.program_id(1) == 0))
    def _():
        carry_ref[...] = jnp.zeros_like(carry_ref)

    h = _mod_norm(x_ref[...], mod_ref, nw_ref)
    h_hi = h.astype(BF16)
    h_lo = (h - h_hi.astype(F32)).astype(BF16)
    rw = rw_ref[...]
    rw_hi = rw.astype(BF16)
    rw_lo = (rw - rw_hi.astype(F32)).astype(BF16)
    logits = (jnp.dot(h_hi, rw_hi, preferred_element_type=F32)
              + (jnp.dot(h_hi, rw_lo, preferred_element_type=F32)
                 + jnp.dot(h_lo, rw_hi, preferred_element_type=F32))) + rb_ref[...]
    idx = lax.broadcasted_iota(jnp.int32, logits.shape, 1).astype(F32)
    neg = jnp.float32(-jnp.inf)
    logits = jnp.where(idx < N_EXPERTS, logits, neg)
    m1 = jnp.max(logits, axis=1, keepdims=True)
    i1 = jnp.min(jnp.where(logits == m1, idx, float(LANES)), axis=1, keepdims=True)
    rest = jnp.where(idx == i1, neg, logits)
    m2 = jnp.max(rest, axis=1, keepdims=True)
    i2 = jnp.min(jnp.where(rest == m2, idx, float(LANES)), axis=1, keepdims=True)
    e2 = jnp.exp(m2 - m1)
    den = 1.0 / (1.0 + e2)
    oh1 = jnp.where(idx == i1, 1.0, 0.0)
    oh2 = jnp.where(idx == i2, 1.0, 0.0)
    oh = oh1 + oh2
    n = h.shape[0]
    r = lax.broadcasted_iota(jnp.int32, (n, n), 0)
    c = lax.broadcasted_iota(jnp.int32, (n, n), 1)
    before = jnp.where(c < r, 1.0, 0.0).astype(BF16)
    cum = jnp.dot(before, oh.astype(BF16), preferred_element_type=F32) + carry_ref[0:1, :]
    rank1 = jnp.sum(cum * oh1, axis=1, keepdims=True)
    rank2 = jnp.sum(cum * oh2, axis=1, keepdims=True)
    carry_ref[...] = carry_ref[...] + jnp.sum(oh, axis=0, keepdims=True)
    cnt_ref[...] = carry_ref[...]
    vals = (i1, i2, rank1, rank2, den, e2 * den)
    route = jnp.zeros_like(logits)
    for k, val in enumerate(vals):
        route = jnp.where(idx == k, val, route)
    route_ref[...] = route
    h_ref[...] = h


def _router_call(xa, mod, nw, rw, rb, nt_all):
    b, t, d = xa.shape
    nt = t // TOK_TILE
    return pl.pallas_call(
        _router_body,
        grid=(b, nt),
        in_specs=[
            pl.BlockSpec((None, TOK_TILE, d), lambda bi, i: (bi, i, 0)),
            pl.BlockSpec((None, None, 6, d), lambda bi, i: (bi, i // (nt_all - 1), 0, 0)),
            pl.BlockSpec((1, d), lambda bi, i: (0, 0)),
            pl.BlockSpec((d, LANES), lambda bi, i: (0, 0)),
            pl.BlockSpec((1, LANES), lambda bi, i: (0, 0)),
        ],
        out_specs=[
            pl.BlockSpec((TOK_TILE, d), lambda bi, i: (bi * nt + i, 0)),
            pl.BlockSpec((None, TOK_TILE, LANES), lambda bi, i: (bi, i, 0)),
            pl.BlockSpec((8, LANES), lambda bi, i: (0, 0)),
        ],
        out_shape=[
            jax.ShapeDtypeStruct((b * t, d), F32),
            jax.ShapeDtypeStruct((b, t, LANES), F32),
            jax.ShapeDtypeStruct((8, LANES), F32),
        ],
        scratch_shapes=[pltpu.VMEM((8, LANES), F32)],
        compiler_params=_cparams(2),
        name="router",
    )(xa, mod, nw, rw, rb)


def _row(ref, r):
    return ref.at[pl.ds(r, 1), :]


def _row_copy_wait(src_ref, dst_ref, sem, count):
    def body(_, carry):
        pltpu.make_async_copy(_row(src_ref, 0), _row(dst_ref, 0), sem).wait()
        return carry
    lax.fori_loop(0, count, body, 0, unroll=8)


def _dispatch_body(dest_ref, h_ref, xs_in_ref, xs_ref, sem):
    del xs_in_ref
    n = h_ref.shape[0]

    def issue(r, carry):
        for k in range(2):
            pltpu.make_async_copy(_row(h_ref, r), _row(xs_ref, dest_ref[0, k * n + r]), sem).start()
        return carry
    lax.fori_loop(0, n, issue, 0, unroll=8)
    _row_copy_wait(h_ref, xs_ref, sem, 2 * n)


def _dispatch_call(dest, h, xs_init):
    n_tok, d = h.shape
    nt = n_tok // TOK_TILE
    return pl.pallas_call(
        _dispatch_body,
        grid=(nt,),
        in_specs=[
            pl.BlockSpec((None, 1, 2 * TOK_TILE), lambda i: (i, 0, 0), memory_space=pltpu.SMEM),
            pl.BlockSpec((TOK_TILE, d), lambda i: (i, 0)),
            pl.BlockSpec(memory_space=pl.ANY),
        ],
        out_specs=pl.BlockSpec(memory_space=pl.ANY),
        out_shape=jax.ShapeDtypeStruct(xs_init.shape, F32),
        scratch_shapes=[pltpu.SemaphoreType.DMA(())],
        input_output_aliases={2: 0},
        compiler_params=_cparams(1),
        name="moe_dispatch",
    )(dest, h, xs_init)


def _grouped_ffn_body(te_ref, nu_ref, xs_ref, wg_ref, wu_ref, wd_ref, ys_ref):
    del te_ref
    used = pl.program_id(0) < nu_ref[0]

    @pl.when(used)
    def _():
        ys_ref[...] = _swiglu(xs_ref[...].astype(BF16), wg_ref, wu_ref, wd_ref)

    @pl.when(jnp.logical_not(used))
    def _():
        ys_ref[...] = jnp.zeros_like(ys_ref)


def _grouped_ffn_call(tile_expert, n_used, xs, wg, wu, wd):
    n_slots = xs.shape[0]
    n_tiles = n_slots // TOK_TILE
    _, d, f = wg.shape

    return pl.pallas_call(
        _grouped_ffn_body,
        grid_spec=pltpu.PrefetchScalarGridSpec(
            num_scalar_prefetch=2,
            grid=(n_tiles,),
            in_specs=[
                pl.BlockSpec((TOK_TILE, d), lambda j, te, nu: (jnp.minimum(j, nu[0] - 1), 0)),
                pl.BlockSpec((None, d, f), lambda j, te, nu: (te[j], 0, 0)),
                pl.BlockSpec((None, d, f), lambda j, te, nu: (te[j], 0, 0)),
                pl.BlockSpec((None, f, d), lambda j, te, nu: (te[j], 0, 0)),
            ],
            out_specs=pl.BlockSpec((TOK_TILE, d), lambda j, te, nu: (j, 0)),
        ),
        out_shape=jax.ShapeDtypeStruct(xs.shape, F32),
        compiler_params=_cparams(1),
        name="grouped_ffn",
    )(tile_expert, n_used, xs, wg, wu, wd)


def _combine_body(dest_ref, x_ref, route_ref, mod_ref, ys_ref, o_ref, buf_ref, sem):
    n = x_ref.shape[0]

    def issue(r, carry):
        for k in range(2):
            pltpu.make_async_copy(_row(ys_ref, dest_ref[0, k * n + r]), _row(buf_ref.at[k], r),
                                  sem).start()
        return carry
    lax.fori_loop(0, n, issue, 0, unroll=8)
    _row_copy_wait(ys_ref, buf_ref.at[0], sem, 2 * n)
    y = route_ref[:, 4:5] * buf_ref[0] + route_ref[:, 5:6] * buf_ref[1]
    o_ref[...] = x_ref[...] + mod_ref[5:6, :] * y


def _combine_call(dest, xa, route, mod, ys, nt_all):
    b, t, d = xa.shape
    nt = t // TOK_TILE
    return pl.pallas_call(
        _combine_body,
        grid=(b, nt),
        in_specs=[
            pl.BlockSpec((None, 1, 2 * TOK_TILE), lambda bi, i: (bi * nt + i, 0, 0),
                         memory_space=pltpu.SMEM),
            pl.BlockSpec((None, TOK_TILE, d), lambda bi, i: (bi, i, 0)),
            pl.BlockSpec((None, TOK_TILE, LANES), lambda bi, i: (bi, i, 0)),
            pl.BlockSpec((None, None, 6, d), lambda bi, i: (bi, i // (nt_all - 1), 0, 0)),
            pl.BlockSpec(memory_space=pl.ANY),
        ],
        out_specs=pl.BlockSpec((None, TOK_TILE, d), lambda bi, i: (bi, i, 0)),
        out_shape=jax.ShapeDtypeStruct((b, t, d), F32),
        scratch_shapes=[pltpu.VMEM((2, TOK_TILE, d), F32), pltpu.SemaphoreType.DMA(())],
        compiler_params=_cparams(2),
        name="moe_combine",
    )(dest, xa, route, mod, ys)


def _moe_layer(xa, mod, nw, rw, rb, wg, wu, wd, nt_all, n_exp, first_expert):
    b, t, d = xa.shape
    n_tok = b * t
    h, route, counts = _router_call(xa, mod, nw, rw, rb, nt_all)
    cnt = counts[0, :n_exp].astype(jnp.int32)
    padded = ((cnt + TOK_TILE - 1) // TOK_TILE) * TOK_TILE
    ends = jnp.cumsum(padded)
    starts = ends - padded
    n_tiles = (2 * n_tok) // TOK_TILE + n_exp
    n_used = (ends[-1] // TOK_TILE).astype(jnp.int32)
    tile_start = jnp.arange(n_tiles, dtype=jnp.int32) * TOK_TILE
    tile_expert = jnp.minimum(jnp.searchsorted(ends, tile_start, side="right"), n_exp - 1)
    tile_expert = tile_expert.astype(jnp.int32) + first_expert
    expert = route[..., 0:2].astype(jnp.int32)
    rank = route[..., 2:4].astype(jnp.int32)
    start = sum(jnp.where(expert == e, starts[e], 0) for e in range(n_exp))
    dest = start + rank
    dest = dest.reshape(b * (t // TOK_TILE), TOK_TILE, 2).transpose(0, 2, 1)
    dest = dest.reshape(b * (t // TOK_TILE), 1, 2 * TOK_TILE)
    xs = _dispatch_call(dest, h, jnp.zeros((n_tiles * TOK_TILE, d), F32))
    ys = _grouped_ffn_call(tile_expert, n_used.reshape(1), xs, wg, wu, wd)
    return _combine_call(dest, xa, route, mod, ys, nt_all)


def _rope_tables(seq, ctx_len):
    pos_row = jnp.arange(seq, dtype=jnp.int32) // GRID_W
    pos_col = jnp.arange(seq, dtype=jnp.int32) % GRID_W

    def tables(width):
        n = width // 2
        half = n // 2
        inv = ROPE_THETA ** (-jnp.arange(0, n, 2, dtype=F32) / n)
        d = jnp.arange(LANES)
        r = d % n
        first = (r < half)[None, :]
        is_col = ((d % width) // n == 1)[None, :]
        pos = jnp.where(is_col, pos_col[:, None], pos_row[:, None]).astype(F32)
        ang = pos * inv[r % half][None, :]
        cos, sin = jnp.cos(ang), jnp.sin(ang)
        tabs = [cos, jnp.where(first, -sin, 0.0), jnp.where(first, 0.0, sin)]
        ident = [jnp.ones((ctx_len, LANES), F32), jnp.zeros((ctx_len, LANES), F32),
                 jnp.zeros((ctx_len, LANES), F32)]
        return [jnp.concatenate([a, b], axis=0) for a, b in zip(tabs, ident)]

    return jnp.stack(tables(HEAD_DIM) + tables(C_QK_DIM), axis=0)


def _lane_tile(vec, scale=1.0):
    return jnp.tile(vec.astype(F32) * scale, LANES // vec.shape[0])


def _score_bound(q_gain, k_gain, dim):
    g = jnp.max(jnp.abs(q_gain.astype(F32))) * jnp.max(jnp.abs(k_gain.astype(F32)))
    return jnp.ceil(math.sqrt(dim) * LOG2E * 1.01 * g) + 1.0


def kernel(x, c, ctx, c_ctx, w_ada, b_ada, norm_mix, norm_ffn, w_in, w_out, a_q_gain, a_k_gain, d_q_gain, d_k_gain, conv_w, diff_lambda, diff_subln, dense_w_gate, dense_w_up, dense_w_down, router_w, router_b, moe_w_gate, moe_w_up, moe_w_down):
    b, seq, d = x.shape
    ctx_len = ctx.shape[1]
    depth = w_ada.shape[0]
    t = seq + ctx_len
    nt_all = t // TOK_TILE
    nt_lat = seq // TOK_TILE

    xa = jnp.concatenate([x, ctx], axis=1)
    rows = ((b + 1 + 7) // 8) * 8
    cond = jnp.concatenate([c, c_ctx[None, :], jnp.zeros((rows - b - 1, d), F32)], axis=0)
    mods = _ada_call(cond, w_ada, b_ada.reshape(depth, 1, 6 * d))
    rope = _rope_tables(seq, ctx_len)

    n_exp = moe_w_gate.shape[1]
    moe_wg = moe_w_gate.reshape((-1,) + moe_w_gate.shape[2:]).astype(BF16)
    moe_wu = moe_w_up.reshape((-1,) + moe_w_up.shape[2:]).astype(BF16)
    moe_wd = moe_w_down.reshape((-1,) + moe_w_down.shape[2:]).astype(BF16)

    lane = jnp.arange(LANES)
    qmask = jnp.stack(
        [lane < 64, lane >= 64] + [(lane // 32) == g for g in range(4)]
        + [lane < 0, lane < 0], axis=0).astype(BF16)

    for l in range(depth):
        last = l == depth - 1
        lam_init = 0.8 - 0.6 * math.exp(-0.3 * l)
        mod = jnp.concatenate(
            [mods[l, :b].reshape(b, 1, 6, d),
             jnp.broadcast_to(mods[l, b].reshape(1, 1, 6, d), (b, 1, 6, d))], axis=1)
        zeros = jnp.zeros((LANES,), F32)
        bound_a = _score_bound(a_q_gain[l], a_k_gain[l], HEAD_DIM)
        bound_c = _score_bound(d_q_gain[l], d_k_gain[l], C_QK_DIM)
        static_ok = jnp.maximum(bound_a, bound_c) <= MAX_STATIC_BOUND
        gains = jnp.stack(
            [_lane_tile(a_q_gain[l], HEAD_DIM ** -0.5 * LOG2E), _lane_tile(a_k_gain[l]),
             _lane_tile(d_q_gain[l], C_QK_DIM ** -0.5 * LOG2E), _lane_tile(d_k_gain[l]),
             zeros - jnp.where(static_ok, bound_a, 0.0), zeros - jnp.where(static_ok, bound_c, 0.0),
             zeros, zeros], axis=0)
        q, kt, v, cv = _pre_attn_call(xa, mod, norm_mix[l][None, :], w_in[l].astype(BF16),
                                      gains, rope)
        zc = jnp.zeros((CONV_CH,), F32)
        misc = jnp.stack(
            [conv_w[l, 0], conv_w[l, 1], conv_w[l, 2],
             jnp.tile(diff_subln[l].astype(F32), CONV_CH // diff_subln.shape[1]),
             zc, zc, zc, zc], axis=0)
        n_tiles = nt_lat if last else nt_all
        attn_args = (xa, q, kt, v, cv, mod, w_out[l].astype(BF16), qmask, misc,
                     diff_lambda[l].astype(F32))
        xa = lax.cond(
            static_ok,
            lambda *a: _attn_call(*a, seq, lam_init, n_tiles, True),
            lambda *a: _attn_call(*a, seq, lam_init, n_tiles, False),
            *attn_args)
        nw = norm_ffn[l][None, :]
        if l % 2 == 0:
            i = l // 2
            xa = _dense_ffn_call(xa, mod, nw, dense_w_gate[i].astype(BF16),
                                 dense_w_up[i].astype(BF16), dense_w_down[i].astype(BF16), nt_all)
        else:
            i = l // 2
            rw = jnp.pad(router_w[i], ((0, 0), (0, LANES - N_EXPERTS)))
            rb = jnp.pad(router_b[i], (0, LANES - N_EXPERTS))[None, :]
            xa = _moe_layer(xa, mod, nw, rw, rb, moe_wg, moe_wu, moe_wd, nt_all,
                            n_exp, i * n_exp)
    return xa[:, :seq]
```

```python
import functools
import math

import jax
import jax.numpy as jnp
from jax import lax
from jax.experimental import pallas as pl
from jax.experimental.pallas import tpu as pltpu

F32 = jnp.float32
BF16 = jnp.bfloat16

GRID_W = 64
HEAD_DIM = 64
A_KV_HEADS = 2
CONV_CH = 256
C_HEADS = 4
C_QK_DIM = 32
N_EXPERTS = 8
ROPE_THETA = 10000.0
EPS = 1e-6
LOG2E = 1.4426950408889634

LANES = 128
TOK_TILE = 256
PRE_TILES = 3
ATT_TILES = 2
KEY_CHUNK = 256
KV_WIDTH = (A_KV_HEADS + C_HEADS // 2) * 2 * LANES
MAX_STATIC_BOUND = 60.0
VMEM_LIMIT = 56 * 1024 * 1024


def _cparams(n_axes):
    return pltpu.CompilerParams(
        dimension_semantics=("arbitrary",) * n_axes, vmem_limit_bytes=VMEM_LIMIT)


def _group_sum_matrix(group):
    r = lax.broadcasted_iota(jnp.int32, (LANES, LANES), 0)
    c = lax.broadcasted_iota(jnp.int32, (LANES, LANES), 1)
    shift = int(math.log2(group))
    same = (r >> shift) == (c >> shift)
    return jnp.where(same, 1.0, 0.0).astype(BF16)


def _group_rms(xc, gmat, group):
    ss = jnp.dot((xc * xc).astype(BF16), gmat, preferred_element_type=F32)
    return xc * lax.rsqrt(ss * (1.0 / group) + EPS)


def _ada_body(s_ref, w_ref, b_ref, o_ref):
    s = s_ref[...]
    s = s * (1.0 / (1.0 + jnp.exp(-s)))
    o_ref[...] = jnp.dot(s.astype(BF16), w_ref[...].astype(BF16),
                         preferred_element_type=F32) + b_ref[...]


def _ada_call(cond, w_ada, b_ada):
    depth, d, n = w_ada.shape
    r = cond.shape[0]
    tn = 1024
    return pl.pallas_call(
        _ada_body,
        grid=(depth, n // tn),
        in_specs=[
            pl.BlockSpec((r, d), lambda l, j: (0, 0)),
            pl.BlockSpec((None, d, tn), lambda l, j: (l, 0, j)),
            pl.BlockSpec((None, 1, tn), lambda l, j: (l, 0, j)),
        ],
        out_specs=pl.BlockSpec((None, r, tn), lambda l, j: (l, 0, j)),
        out_shape=jax.ShapeDtypeStruct((depth, r, n), F32),
        compiler_params=_cparams(2),
        name="ada_mod",
    )(cond, w_ada, b_ada)


def _pre_attn_body(n_lat, x_ref, mod_ref, nw_ref, w_ref, gain_ref, rope_ref,
                   q_ref, kt_ref, v_ref, cv_ref):
    tq = TOK_TILE
    n_sub = x_ref.shape[0] // tq
    g64 = _group_sum_matrix(64)
    g32 = _group_sum_matrix(32)
    lane = lax.broadcasted_iota(jnp.int32, (tq, LANES), 1)
    krow = lax.broadcasted_iota(jnp.int32, (LANES, tq), 0)

    def bound_block(r):
        return jnp.where(krow == 0, gain_ref[r:r + 1, 0:1], 0.0).astype(BF16)

    for sub in range(n_sub):
        rows = slice(sub * tq, (sub + 1) * tq)
        is_ctx = pl.program_id(1) * n_sub + sub >= n_lat
        shift = jnp.where(is_ctx, mod_ref[1, 0:1, :], mod_ref[0, 0:1, :])
        scale = jnp.where(is_ctx, mod_ref[1, 1:2, :], mod_ref[0, 1:2, :])
        x = x_ref[rows, :]
        h = x * lax.rsqrt(jnp.mean(x * x, axis=-1, keepdims=True) + EPS) * nw_ref[...]
        h = h * (1.0 + scale) + shift
        p = jnp.dot(h.astype(BF16), w_ref[...], preferred_element_type=F32)

        def rope(xn, t0, half, rows=rows):
            return (xn * rope_ref[t0, rows, :]
                    + pltpu.roll(xn, LANES - half, 1) * rope_ref[t0 + 1, rows, :]
                    + pltpu.roll(xn, half, 1) * rope_ref[t0 + 2, rows, :])

        def chunk(col, p=p):
            return p[:, col:col + LANES]

        for c in range(4):
            xn = _group_rms(chunk(c * LANES), g64, 64) * gain_ref[0:1, :]
            q_ref[rows, c * LANES:(c + 1) * LANES] = rope(xn, 0, 16).astype(BF16)
        ka = rope(_group_rms(chunk(512), g64, 64) * gain_ref[1:2, :], 0, 16)
        kat = ka.T.astype(BF16)
        for kv in range(A_KV_HEADS):
            r0 = kv * 2 * LANES
            kt_ref[r0:r0 + 64, rows] = kat[kv * 64:(kv + 1) * 64]
            kt_ref[r0 + 64:r0 + LANES, rows] = kat[kv * 64:(kv + 1) * 64]
            kt_ref[r0 + LANES:r0 + 2 * LANES, rows] = bound_block(4)
        av = chunk(640)
        av_sw = pltpu.roll(av, 64, 1)
        v_ref[rows, 0:LANES] = jnp.where(lane < 64, av, 1.0).astype(BF16)
        v_ref[rows, LANES:2 * LANES] = jnp.where(lane < 64, 1.0, av_sw).astype(BF16)
        v_ref[rows, 2 * LANES:3 * LANES] = jnp.where(lane < 64, av_sw, 1.0).astype(BF16)
        v_ref[rows, 3 * LANES:4 * LANES] = jnp.where(lane < 64, 1.0, av).astype(BF16)
        cv_ref[rows, 0:CONV_CH] = p[:, 1280:1536] * p[:, 768:1024]
        cv_ref[rows, CONV_CH:2 * CONV_CH] = p[:, 1024:1280]
        for c in range(2):
            xn = _group_rms(chunk(1536 + c * LANES), g32, 32) * gain_ref[2:3, :]
            q_ref[rows, 512 + c * LANES:512 + (c + 1) * LANES] = rope(xn, 3, 8).astype(BF16)
            kc = rope(_group_rms(chunk(1792 + c * LANES), g32, 32) * gain_ref[3:4, :], 3, 8)
            r0 = (A_KV_HEADS + c) * 2 * LANES
            kt_ref[r0:r0 + LANES, rows] = kc.T.astype(BF16)
            kt_ref[r0 + LANES:r0 + 2 * LANES, rows] = bound_block(5)
            dv = chunk(2048 + c * LANES)
            v_ref[rows, r0:r0 + LANES] = jnp.where(lane < 64, dv, 1.0).astype(BF16)
            v_ref[rows, r0 + LANES:r0 + 2 * LANES] = jnp.where(lane < 64, 1.0, dv).astype(BF16)


def _pre_attn_call(xa, mod, nw, w_in, gains, rope, seq):
    b, t, d = xa.shape
    tile = PRE_TILES * TOK_TILE
    n_in = w_in.shape[1]
    return pl.pallas_call(
        functools.partial(_pre_attn_body, seq // TOK_TILE),
        grid=(b, t // tile),
        in_specs=[
            pl.BlockSpec((None, tile, d), lambda bi, i: (bi, i, 0)),
            pl.BlockSpec((None, 2, 6, d), lambda bi, i: (bi, 0, 0, 0)),
            pl.BlockSpec((1, d), lambda bi, i: (0, 0)),
            pl.BlockSpec((d, n_in), lambda bi, i: (0, 0)),
            pl.BlockSpec((8, LANES), lambda bi, i: (0, 0)),
            pl.BlockSpec((6, tile, LANES), lambda bi, i: (0, i, 0)),
        ],
        out_specs=[
            pl.BlockSpec((None, tile, 768), lambda bi, i: (bi, i, 0)),
            pl.BlockSpec((None, KV_WIDTH, tile), lambda bi, i: (bi, 0, i)),
            pl.BlockSpec((None, tile, KV_WIDTH), lambda bi, i: (bi, i, 0)),
            pl.BlockSpec((None, tile, 512), lambda bi, i: (bi, i, 0)),
        ],
        out_shape=[
            jax.ShapeDtypeStruct((b, t, 768), BF16),
            jax.ShapeDtypeStruct((b, KV_WIDTH, t), BF16),
            jax.ShapeDtypeStruct((b, t, KV_WIDTH), BF16),
            jax.ShapeDtypeStruct((b, t, 512), F32),
        ],
        compiler_params=_cparams(2),
        name="pre_attn",
    )(xa, mod, nw, w_in, gains, rope)


def _attn_body(seq, lam_init, static_bound, q_ref, kt_ref, v_ref, cv_ref, cvp_ref, cvn_ref, x_ref,
               mod_ref, wo_ref, qmask_ref, misc_ref, lamp_ref, o_ref):
    i = pl.program_id(1)
    t = kt_ref.shape[1]
    tq = TOK_TILE
    n_sub = q_ref.shape[0] // tq
    n_lat_steps = seq // (n_sub * tq)
    lane = lax.broadcasted_iota(jnp.int32, (tq, LANES), 1)
    lamp = lamp_ref[...]
    lam = (jnp.exp(jnp.sum(lamp[0:1] * lamp[1:2], axis=1, keepdims=True))
           - jnp.exp(jnp.sum(lamp[2:3] * lamp[3:4], axis=1, keepdims=True)) + lam_init)
    unit = jnp.where(lane == 0, 1.0, 0.0).astype(BF16)

    def masked_q(rows, col, mask_row):
        qc = q_ref[rows, col:col + LANES] * qmask_ref[mask_row:mask_row + 1, :]
        return jnp.concatenate([qc, unit], axis=1)

    def attend_pair(lhs_pair, blk, half, k0, nk):
        r0 = blk * 2 * LANES
        vc = r0 + half * LANES
        if static_bound:
            lhs = jnp.concatenate(lhs_pair, axis=0)
            acc = None
            for kc in range(k0, k0 + nk, KEY_CHUNK):
                s = jnp.dot(lhs, kt_ref[r0:r0 + 2 * LANES, kc:kc + KEY_CHUNK],
                            preferred_element_type=F32)
                pv = jnp.dot(jnp.exp2(s).astype(BF16), v_ref[kc:kc + KEY_CHUNK, vc:vc + LANES],
                             preferred_element_type=F32)
                acc = pv if acc is None else acc + pv
            return acc
        outs = []
        for lhs in lhs_pair:
            s = jnp.dot(lhs, kt_ref[r0:r0 + 2 * LANES, k0:k0 + nk], preferred_element_type=F32)
            e = jnp.exp2(s - jnp.max(s, axis=1, keepdims=True)).astype(BF16)
            outs.append(jnp.dot(e, v_ref[k0:k0 + nk, vc:vc + LANES], preferred_element_type=F32))
        return jnp.concatenate(outs, axis=0)

    def attend(lhs_list, blk, k0, nk):
        return (attend_pair(lhs_list[0:2], blk, 0, k0, nk),
                attend_pair(lhs_list[2:4], blk, 1, k0, nk))

    def normalise(low, high):
        num = jnp.where(lane < 64, low, high)
        den = jnp.where(lane < 64, pltpu.roll(low, 64, 1), pltpu.roll(high, 64, 1))
        return num / den

    def mixers(rows, k0, nk, stream, u_before, u_after):
        ys = []
        for kv in range(A_KV_HEADS):
            cols = (2 * kv * LANES, (2 * kv + 1) * LANES)
            low, high = attend([masked_q(rows, cols[0], 0), masked_q(rows, cols[1], 0),
                                masked_q(rows, cols[0], 1), masked_q(rows, cols[1], 1)],
                               kv, k0, nk)
            for n in range(2):
                ys.append(normalise(low[n * tq:(n + 1) * tq], high[n * tq:(n + 1) * tq]))
        u = cv_ref[rows, 0:CONV_CH]
        row = lax.broadcasted_iota(jnp.int32, (tq, CONV_CH), 0)
        u_prev = jnp.where(row == 0, u_before, pltpu.roll(u, 1, 0))
        u_next = jnp.where(row == tq - 1, u_after, pltpu.roll(u, tq - 1, 0))
        conv = (misc_ref[0:1, :] * u_prev + misc_ref[1:2, :] * u + misc_ref[2:3, :] * u_next)
        b_out = cv_ref[rows, CONV_CH:2 * CONV_CH] * conv
        ys.append(b_out[:, 0:LANES])
        ys.append(b_out[:, LANES:2 * LANES])
        g64 = _group_sum_matrix(64)
        for c in range(C_HEADS // 2):
            col = 512 + c * LANES
            low, high = attend([masked_q(rows, col, 2 + g) for g in range(4)],
                               A_KV_HEADS + c, k0, nk)
            yc = (normalise(low[0:tq], high[0:tq])
                  - lam * normalise(low[tq:2 * tq], high[tq:2 * tq]))
            ys.append(_group_rms(yc, g64, 64) * misc_ref[3:4, 0:LANES] * (1.0 - lam_init))
        y = jnp.concatenate(ys, axis=1).astype(BF16)
        o = jnp.dot(y, wo_ref[...], preferred_element_type=F32)
        o_ref[rows, :] = x_ref[rows, :] + mod_ref[stream, 2:3, :] * o

    @pl.when(i < n_lat_steps)
    def _():
        for sub in range(n_sub):
            r0 = sub * tq
            if sub == 0:
                u_before = cvp_ref[7:8, 0:CONV_CH] * jnp.where(i > 0, 1.0, 0.0)
            else:
                u_before = cv_ref[r0 - 1:r0, 0:CONV_CH]
            if sub == n_sub - 1:
                u_after = cvn_ref[0:1, 0:CONV_CH] * jnp.where(i < n_lat_steps - 1, 1.0, 0.0)
            else:
                u_after = cv_ref[r0 + tq:r0 + tq + 1, 0:CONV_CH]
            mixers(slice(r0, r0 + tq), 0, t, 0, u_before, u_after)

    @pl.when(i >= n_lat_steps)
    def _():
        zero = jnp.zeros((1, CONV_CH), F32)
        mixers(slice(0, tq), seq, t - seq, 1, zero, zero)


def _attn_call(xa, q, kt, v, cv, mod, wo, qmask, misc, lamp, seq, lam_init, n_tiles, static_bound):
    b, t, d = xa.shape
    tile = ATT_TILES * TOK_TILE
    n_lat = seq // TOK_TILE
    assert seq % tile == 0 and t - seq == TOK_TILE and n_tiles in (n_lat, n_lat + 1)
    n_steps = seq // tile + (n_tiles - n_lat)
    n8 = t // 8
    blk8 = tile // 8
    return pl.pallas_call(
        functools.partial(_attn_body, seq, lam_init, static_bound),
        grid=(b, n_steps),
        in_specs=[
            pl.BlockSpec((None, tile, 768), lambda bi, i: (bi, i, 0)),
            pl.BlockSpec((None, KV_WIDTH, t), lambda bi, i: (bi, 0, 0)),
            pl.BlockSpec((None, t, KV_WIDTH), lambda bi, i: (bi, 0, 0)),
            pl.BlockSpec((None, tile, 512), lambda bi, i: (bi, i, 0)),
            pl.BlockSpec((None, 8, 512), lambda bi, i: (bi, jnp.maximum(i * blk8 - 1, 0), 0)),
            pl.BlockSpec((None, 8, 512), lambda bi, i: (bi, jnp.minimum((i + 1) * blk8, n8 - 1), 0)),
            pl.BlockSpec((None, tile, d), lambda bi, i: (bi, i, 0)),
            pl.BlockSpec((None, 2, 6, d), lambda bi, i: (bi, 0, 0, 0)),
            pl.BlockSpec((d, d), lambda bi, i: (0, 0)),
            pl.BlockSpec((8, LANES), lambda bi, i: (0, 0)),
            pl.BlockSpec((8, CONV_CH), lambda bi, i: (0, 0)),
            pl.BlockSpec((4, C_QK_DIM), lambda bi, i: (0, 0)),
        ],
        out_specs=pl.BlockSpec((None, tile, d), lambda bi, i: (bi, i, 0)),
        out_shape=jax.ShapeDtypeStruct((b, n_tiles * TOK_TILE, d), F32),
        compiler_params=_cparams(2),
        name="attn_merge",
    )(q, kt, v, cv, cv, cv, xa, mod, wo, qmask, misc, lamp)


def _mod_norm(x, mod_ref, nw_ref):
    h = x * lax.rsqrt(jnp.mean(x * x, axis=-1, keepdims=True) + EPS) * nw_ref[...]
    return h * (1.0 + mod_ref[4:5, :]) + mod_ref[3:4, :]


def _swiglu(h, wg_ref, wu_ref, wd_ref):
    g = jnp.dot(h, wg_ref[...], preferred_element_type=F32)
    u = jnp.dot(h, wu_ref[...], preferred_element_type=F32)
    a = g * (1.0 / (1.0 + jnp.exp(-g))) * u
    return jnp.dot(a.astype(BF16), wd_ref[...], preferred_element_type=F32)


def _dense_ffn_body(n_lat, x_ref, mod_ref, nw_ref, wg_ref, wu_ref, wd_ref, o_ref):
    tq = TOK_TILE
    n_sub = x_ref.shape[0] // tq
    for sub in range(n_sub):
        rows = slice(sub * tq, (sub + 1) * tq)
        m_ref = mod_ref.at[jnp.where(pl.program_id(1) * n_sub + sub >= n_lat, 1, 0)]
        x = x_ref[rows, :]
        h = _mod_norm(x, m_ref, nw_ref).astype(BF16)
        o_ref[rows, :] = x + m_ref[5:6, :] * _swiglu(h, wg_ref, wu_ref, wd_ref)


def _dense_ffn_call(xa, mod, nw, wg, wu, wd, seq):
    b, t, d = xa.shape
    nt = t // TOK_TILE
    n_sub = max(k for k in range(1, PRE_TILES + 1) if nt % k == 0)
    tile = n_sub * TOK_TILE
    f = wg.shape[1]
    return pl.pallas_call(
        functools.partial(_dense_ffn_body, seq // TOK_TILE),
        grid=(b, nt // n_sub),
        in_specs=[
            pl.BlockSpec((None, tile, d), lambda bi, i: (bi, i, 0)),
            pl.BlockSpec((None, 2, 6, d), lambda bi, i: (bi, 0, 0, 0)),
            pl.BlockSpec((1, d), lambda bi, i: (0, 0)),
            pl.BlockSpec((d, f), lambda bi, i: (0, 0)),
            pl.BlockSpec((d, f), lambda bi, i: (0, 0)),
            pl.BlockSpec((f, d), lambda bi, i: (0, 0)),
        ],
        out_specs=pl.BlockSpec((None, tile, d), lambda bi, i: (bi, i, 0)),
        out_shape=jax.ShapeDtypeStruct((b, t, d), F32),
        compiler_params=_cparams(2),
        name="dense_ffn",
    )(xa, mod, nw, wg, wu, wd)


def _router_body(x_ref, mod_ref, nw_ref, rw_ref, rb_ref, h_ref, route_ref, cnt_ref, carry_ref):
    @pl.when((pl.program_id(0) == 0) & (pl.program_id(1) == 0))
    def _():
        carry_ref[...] = jnp.zeros_like(carry_ref)

    h = _mod_norm(x_ref[...], mod_ref, nw_ref)
    h_hi = h.astype(BF16)
    h_lo = (h - h_hi.astype(F32)).astype(BF16)
    rw = rw_ref[...]
    rw_hi = rw.astype(BF16)
    rw_lo = (rw - rw_hi.astype(F32)).astype(BF16)
    logits = (jnp.dot(h_hi, rw_hi, preferred_element_type=F32)
              + (jnp.dot(h_hi, rw_lo, preferred_element_type=F32)
                 + jnp.dot(h_lo, rw_hi, preferred_element_type=F32))) + rb_ref[...]
    idx = lax.broadcasted_iota(jnp.int32, logits.shape, 1).astype(F32)
    neg = jnp.float32(-jnp.inf)
    logits = jnp.where(idx < N_EXPERTS, logits, neg)
    m1 = jnp.max(logits, axis=1, keepdims=True)
    i1 = jnp.min(jnp.where(logits == m1, idx, float(LANES)), axis=1, keepdims=True)
    rest = jnp.where(idx == i1, neg, logits)
    m2 = jnp.max(rest, axis=1, keepdims=True)
    i2 = jnp.min(jnp.where(rest == m2, idx, float(LANES)), axis=1, keepdims=True)
    e2 = jnp.exp(m2 - m1)
    den = 1.0 / (1.0 + e2)
    oh1 = jnp.where(idx == i1, 1.0, 0.0)
    oh2 = jnp.where(idx == i2, 1.0, 0.0)
    oh = oh1 + oh2
    n = h.shape[0]
    r = lax.broadcasted_iota(jnp.int32, (n, n), 0)
    c = lax.broadcasted_iota(jnp.int32, (n, n), 1)
    before = jnp.where(c < r, 1.0, 0.0).astype(BF16)
    cum = jnp.dot(before, oh.astype(BF16), preferred_element_type=F32) + carry_ref[0:1, :]
    rank1 = jnp.sum(cum * oh1, axis=1, keepdims=True)
    rank2 = jnp.sum(cum * oh2, axis=1, keepdims=True)
    carry_ref[...] = carry_ref[...] + jnp.sum(oh, axis=0, keepdims=True)
    cnt_ref[...] = carry_ref[...]
    vals = (i1, i2, rank1, rank2, den, e2 * den)
    route = jnp.zeros_like(logits)
    for k, val in enumerate(vals):
        route = jnp.where(idx == k, val, route)
    route_ref[...] = route
    h_ref[...] = h


def _router_call(xa, mod, nw, rw, rb, nt_all):
    b, t, d = xa.shape
    nt = t // TOK_TILE
    return pl.pallas_call(
        _router_body,
        grid=(b, nt),
        in_specs=[
            pl.BlockSpec((None, TOK_TILE, d), lambda bi, i: (bi, i, 0)),
            pl.BlockSpec((None, None, 6, d), lambda bi, i: (bi, i // (nt_all - 1), 0, 0)),
            pl.BlockSpec((1, d), lambda bi, i: (0, 0)),
            pl.BlockSpec((d, LANES), lambda bi, i: (0, 0)),
            pl.BlockSpec((1, LANES), lambda bi, i: (0, 0)),
        ],
        out_specs=[
            pl.BlockSpec((TOK_TILE, d), lambda bi, i: (bi * nt + i, 0)),
            pl.BlockSpec((None, TOK_TILE, LANES), lambda bi, i: (bi, i, 0)),
            pl.BlockSpec((8, LANES), lambda bi, i: (0, 0)),
        ],
        out_shape=[
            jax.ShapeDtypeStruct((b * t, d), F32),
            jax.ShapeDtypeStruct((b, t, LANES), F32),
            jax.ShapeDtypeStruct((8, LANES), F32),
        ],
        scratch_shapes=[pltpu.VMEM((8, LANES), F32)],
        compiler_params=_cparams(2),
        name="router",
    )(xa, mod, nw, rw, rb)


def _row(ref, r):
    return ref.at[pl.ds(r, 1), :]


def _row_copy_wait(src_ref, dst_ref, sem, count):
    def body(_, carry):
        pltpu.make_async_copy(_row(src_ref, 0), _row(dst_ref, 0), sem).wait()
        return carry
    lax.fori_loop(0, count, body, 0, unroll=8)


def _dispatch_body(dest_ref, pad_ref, h_ref, xs_ref, zero_ref, sem):
    n = h_ref.shape[0]

    @pl.when(pl.program_id(0) == 0)
    def _():
        zero_ref[...] = jnp.zeros_like(zero_ref)
        n_pad = pad_ref.shape[1]

        def issue_pad(j, carry):
            pltpu.make_async_copy(_row(zero_ref, 0), _row(xs_ref, pad_ref[0, j]), sem).start()
            return carry
        lax.fori_loop(0, n_pad, issue_pad, 0, unroll=8)
        _row_copy_wait(zero_ref, xs_ref, sem, n_pad)

    def issue(r, carry):
        for k in range(2):
            pltpu.make_async_copy(_row(h_ref, r), _row(xs_ref, dest_ref[0, k * n + r]), sem).start()
        return carry
    lax.fori_loop(0, n, issue, 0, unroll=8)
    _row_copy_wait(h_ref, xs_ref, sem, 2 * n)


def _dispatch_call(dest, pad, h, n_slots):
    n_tok, d = h.shape
    nt = n_tok // TOK_TILE
    return pl.pallas_call(
        _dispatch_body,
        grid=(nt,),
        in_specs=[
            pl.BlockSpec((None, 1, 2 * TOK_TILE), lambda i: (i, 0, 0), memory_space=pltpu.SMEM),
            pl.BlockSpec((1, pad.shape[1]), lambda i: (0, 0), memory_space=pltpu.SMEM),
            pl.BlockSpec((TOK_TILE, d), lambda i: (i, 0)),
        ],
        out_specs=pl.BlockSpec(memory_space=pl.ANY),
        out_shape=jax.ShapeDtypeStruct((n_slots, d), F32),
        scratch_shapes=[pltpu.VMEM((8, d), F32), pltpu.SemaphoreType.DMA(())],
        compiler_params=_cparams(1),
        name="moe_dispatch",
    )(dest, pad, h)


def _grouped_ffn_body(te_ref, nu_ref, xs_ref, wg_ref, wu_ref, wd_ref, ys_ref):
    del te_ref
    used = pl.program_id(0) < nu_ref[0]

    @pl.when(used)
    def _():
        ys_ref[...] = _swiglu(xs_ref[...].astype(BF16), wg_ref, wu_ref, wd_ref)

    @pl.when(jnp.logical_not(used))
    def _():
        ys_ref[...] = jnp.zeros_like(ys_ref)


def _grouped_ffn_call(tile_expert, n_used, xs, wg, wu, wd):
    n_slots = xs.shape[0]
    n_tiles = n_slots // TOK_TILE
    _, d, f = wg.shape

    return pl.pallas_call(
        _grouped_ffn_body,
        grid_spec=pltpu.PrefetchScalarGridSpec(
            num_scalar_prefetch=2,
            grid=(n_tiles,),
            in_specs=[
                pl.BlockSpec((TOK_TILE, d), lambda j, te, nu: (jnp.minimum(j, nu[0] - 1), 0)),
                pl.BlockSpec((None, d, f), lambda j, te, nu: (te[j], 0, 0)),
                pl.BlockSpec((None, d, f), lambda j, te, nu: (te[j], 0, 0)),
                pl.BlockSpec((None, f, d), lambda j, te, nu: (te[j], 0, 0)),
            ],
            out_specs=pl.BlockSpec((TOK_TILE, d), lambda j, te, nu: (j, 0)),
        ),
        out_shape=jax.ShapeDtypeStruct(xs.shape, F32),
        compiler_params=_cparams(1),
        name="grouped_ffn",
    )(tile_expert, n_used, xs, wg, wu, wd)


def _combine_body(dest_ref, dest_next_ref, x_ref, route_ref, mod_ref, ys_ref, o_ref, buf_ref, sems):
    n = x_ref.shape[0]
    step = pl.program_id(0) * pl.num_programs(1) + pl.program_id(1)
    last = pl.num_programs(0) * pl.num_programs(1) - 1
    cur = lax.rem(step, 2)

    def gather(d_ref, half):
        def issue(r, carry):
            for k in range(2):
                pltpu.make_async_copy(_row(ys_ref, d_ref[0, k * n + r]),
                                      _row(buf_ref.at[half, k], r), sems.at[half]).start()
            return carry
        lax.fori_loop(0, n, issue, 0, unroll=8)

    @pl.when(step == 0)
    def _():
        gather(dest_ref, 0)

    @pl.when(step < last)
    def _():
        gather(dest_next_ref, 1 - cur)

    _row_copy_wait(ys_ref, buf_ref.at[cur, 0], sems.at[cur], 2 * n)
    y = route_ref[:, 4:5] * buf_ref[cur, 0] + route_ref[:, 5:6] * buf_ref[cur, 1]
    o_ref[...] = x_ref[...] + mod_ref[5:6, :] * y


def _combine_call(dest, xa, route, mod, ys, nt_all):
    b, t, d = xa.shape
    nt = t // TOK_TILE
    return pl.pallas_call(
        _combine_body,
        grid=(b, nt),
        in_specs=[
            pl.BlockSpec((None, 1, 2 * TOK_TILE), lambda bi, i: (bi * nt + i, 0, 0),
                         memory_space=pltpu.SMEM),
            pl.BlockSpec((None, 1, 2 * TOK_TILE),
                         lambda bi, i: (jnp.minimum(bi * nt + i + 1, b * nt - 1), 0, 0),
                         memory_space=pltpu.SMEM),
            pl.BlockSpec((None, TOK_TILE, d), lambda bi, i: (bi, i, 0)),
            pl.BlockSpec((None, TOK_TILE, LANES), lambda bi, i: (bi, i, 0)),
            pl.BlockSpec((None, None, 6, d), lambda bi, i: (bi, i // (nt_all - 1), 0, 0)),
            pl.BlockSpec(memory_space=pl.ANY),
        ],
        out_specs=pl.BlockSpec((None, TOK_TILE, d), lambda bi, i: (bi, i, 0)),
        out_shape=jax.ShapeDtypeStruct((b, t, d), F32),
        scratch_shapes=[pltpu.VMEM((2, 2, TOK_TILE, d), F32), pltpu.SemaphoreType.DMA((2,))],
        compiler_params=_cparams(2),
        name="moe_combine",
    )(dest, dest, xa, route, mod, ys)


def _moe_layer(xa, mod, nw, rw, rb, wg, wu, wd, nt_all, n_exp, first_expert):
    b, t, d = xa.shape
    n_tok = b * t
    h, route, counts = _router_call(xa, mod, nw, rw, rb, nt_all)
    cnt = counts[0, :n_exp].astype(jnp.int32)
    padded = ((cnt + TOK_TILE - 1) // TOK_TILE) * TOK_TILE
    ends = jnp.cumsum(padded)
    starts = ends - padded
    n_tiles = (2 * n_tok) // TOK_TILE + n_exp
    n_used = (ends[-1] // TOK_TILE).astype(jnp.int32)
    tile_start = jnp.arange(n_tiles, dtype=jnp.int32) * TOK_TILE
    tile_expert = jnp.minimum(jnp.searchsorted(ends, tile_start, side="right"), n_exp - 1)
    tile_expert = tile_expert.astype(jnp.int32) + first_expert
    expert = route[..., 0:2].astype(jnp.int32)
    rank = route[..., 2:4].astype(jnp.int32)
    start = sum(jnp.where(expert == e, starts[e], 0) for e in range(n_exp))
    dest = start + rank
    dest = dest.reshape(b * (t // TOK_TILE), TOK_TILE, 2).transpose(0, 2, 1)
    dest = dest.reshape(b * (t // TOK_TILE), 1, 2 * TOK_TILE)
    j = jnp.arange(TOK_TILE, dtype=jnp.int32)[None, :]
    n_fill = (padded - cnt)[:, None]
    spare_before = (jnp.cumsum(TOK_TILE - (padded - cnt)) - (TOK_TILE - (padded - cnt)))[:, None]
    pad = jnp.where(j < n_fill, (starts + cnt)[:, None] + j, ends[-1] + spare_before + j - n_fill)
    xs = _dispatch_call(dest, pad.reshape(1, n_exp * TOK_TILE), h, n_tiles * TOK_TILE)
    ys = _grouped_ffn_call(tile_expert, n_used.reshape(1), xs, wg, wu, wd)
    return _combine_call(dest, xa, route, mod, ys, nt_all)


def _rope_tables(seq, ctx_len):
    pos_row = jnp.arange(seq, dtype=jnp.int32) // GRID_W
    pos_col = jnp.arange(seq, dtype=jnp.int32) % GRID_W

    def tables(width):
        n = width // 2
        half = n // 2
        inv = ROPE_THETA ** (-jnp.arange(0, n, 2, dtype=F32) / n)
        d = jnp.arange(LANES)
        r = d % n
        first = (r < half)[None, :]
        is_col = ((d % width) // n == 1)[None, :]
        pos = jnp.where(is_col, pos_col[:, None], pos_row[:, None]).astype(F32)
        ang = pos * inv[r % half][None, :]
        cos, sin = jnp.cos(ang), jnp.sin(ang)
        tabs = [cos, jnp.where(first, -sin, 0.0), jnp.where(first, 0.0, sin)]
        ident = [jnp.ones((ctx_len, LANES), F32), jnp.zeros((ctx_len, LANES), F32),
                 jnp.zeros((ctx_len, LANES), F32)]
        return [jnp.concatenate([a, b], axis=0) for a, b in zip(tabs, ident)]

    return jnp.stack(tables(HEAD_DIM) + tables(C_QK_DIM), axis=0)


def _lane_tile(vec, scale=1.0):
    return jnp.tile(vec.astype(F32) * scale, LANES // vec.shape[0])


def _score_bound(q_gain, k_gain, dim):
    g = jnp.max(jnp.abs(q_gain.astype(F32))) * jnp.max(jnp.abs(k_gain.astype(F32)))
    return jnp.ceil(math.sqrt(dim) * LOG2E * 1.01 * g) + 1.0


def kernel(x, c, ctx, c_ctx, w_ada, b_ada, norm_mix, norm_ffn, w_in, w_out, a_q_gain, a_k_gain, d_q_gain, d_k_gain, conv_w, diff_lambda, diff_subln, dense_w_gate, dense_w_up, dense_w_down, router_w, router_b, moe_w_gate, moe_w_up, moe_w_down):
    b, seq, d = x.shape
    ctx_len = ctx.shape[1]
    depth = w_ada.shape[0]
    t = seq + ctx_len
    nt_all = t // TOK_TILE
    nt_lat = seq // TOK_TILE

    xa = jnp.concatenate([x, ctx], axis=1)
    rows = ((b + 1 + 7) // 8) * 8
    cond = jnp.concatenate([c, c_ctx[None, :], jnp.zeros((rows - b - 1, d), F32)], axis=0)
    mods = _ada_call(cond, w_ada, b_ada.reshape(depth, 1, 6 * d))
    rope = _rope_tables(seq, ctx_len)

    n_exp = moe_w_gate.shape[1]
    moe_wg = moe_w_gate.reshape((-1,) + moe_w_gate.shape[2:]).astype(BF16)
    moe_wu = moe_w_up.reshape((-1,) + moe_w_up.shape[2:]).astype(BF16)
    moe_wd = moe_w_down.reshape((-1,) + moe_w_down.shape[2:]).astype(BF16)

    lane = jnp.arange(LANES)
    qmask = jnp.stack(
        [lane < 64, lane >= 64] + [(lane // 32) == g for g in range(4)]
        + [lane < 0, lane < 0], axis=0).astype(BF16)

    for l in range(depth):
        last = l == depth - 1
        lam_init = 0.8 - 0.6 * math.exp(-0.3 * l)
        mod = jnp.concatenate(
            [mods[l, :b].reshape(b, 1, 6, d),
             jnp.broadcast_to(mods[l, b].reshape(1, 1, 6, d), (b, 1, 6, d))], axis=1)
        zeros = jnp.zeros((LANES,), F32)
        bound_a = _score_bound(a_q_gain[l], a_k_gain[l], HEAD_DIM)
        bound_c = _score_bound(d_q_gain[l], d_k_gain[l], C_QK_DIM)
        static_ok = jnp.maximum(bound_a, bound_c) <= MAX_STATIC_BOUND
        gains = jnp.stack(
            [_lane_tile(a_q_gain[l], HEAD_DIM ** -0.5 * LOG2E), _lane_tile(a_k_gain[l]),
             _lane_tile(d_q_gain[l], C_QK_DIM ** -0.5 * LOG2E), _lane_tile(d_k_gain[l]),
             zeros - jnp.where(static_ok, bound_a, 0.0), zeros - jnp.where(static_ok, bound_c, 0.0),
             zeros, zeros], axis=0)
        q, kt, v, cv = _pre_attn_call(xa, mod, norm_mix[l][None, :], w_in[l].astype(BF16),
                                      gains, rope, seq)
        zc = jnp.zeros((CONV_CH,), F32)
        misc = jnp.stack(
            [conv_w[l, 0], conv_w[l, 1], conv_w[l, 2],
             jnp.tile(diff_subln[l].astype(F32), CONV_CH // diff_subln.shape[1]),
             zc, zc, zc, zc], axis=0)
        n_tiles = nt_lat if last else nt_all
        attn_args = (xa, q, kt, v, cv, mod, w_out[l].astype(BF16), qmask, misc,
                     diff_lambda[l].astype(F32))
        xa = lax.cond(
            static_ok,
            lambda *a: _attn_call(*a, seq, lam_init, n_tiles, True),
            lambda *a: _attn_call(*a, seq, lam_init, n_tiles, False),
            *attn_args)
        nw = norm_ffn[l][None, :]
        if l % 2 == 0:
            i = l // 2
            xa = _dense_ffn_call(xa, mod, nw, dense_w_gate[i].astype(BF16),
                                 dense_w_up[i].astype(BF16), dense_w_down[i].astype(BF16), seq)
        else:
            i = l // 2
            rw = jnp.pad(router_w[i], ((0, 0), (0, LANES - N_EXPERTS)))
            rb = jnp.pad(router_b[i], (0, LANES - N_EXPERTS))[None, :]
            xa = _moe_layer(xa, mod, nw, rw, rb, moe_wg, moe_wu, moe_wd, nt_all,
                            n_exp, i * n_exp)
    return xa[:, :seq]
```
